```python
import math
import jax, jax.numpy as jnp
from jax import lax
import numpy as np

D_MODEL = 1024
BATCH = 16
SEQ = 2048
DEPTH = 2

D_FF = 2816
HEAD_DIM_A = 64
N_HEADS_A = 8
N_KV_HEADS_A = 2
GROUP_A = N_HEADS_A // N_KV_HEADS_A
WINDOW = 128
BLOCK = 128
N_HEADS_B = 8
Q_LORA_RANK = 256
KV_LORA_RANK = 128
QK_NOPE_DIM = 64
QK_ROPE_DIM = 32
QK_DIM_B = QK_NOPE_DIM + QK_ROPE_DIM
V_DIM_B = 64
ROPE_BASE = 10000.0
WIDTH_A = N_HEADS_A * HEAD_DIM_A
WIDTH_B = N_HEADS_B * V_DIM_B
IN_SPLITS = (WIDTH_A, N_KV_HEADS_A * HEAD_DIM_A, N_KV_HEADS_A * HEAD_DIM_A,
             Q_LORA_RANK, KV_LORA_RANK, QK_ROPE_DIM, D_MODEL, D_MODEL)
IN_WIDTH = sum(IN_SPLITS)
IN_OFFSETS = tuple(int(o) for o in np.cumsum(IN_SPLITS)[:-1])
EPS = 1e-6
NEG = -1e30

kernel_name = "hybrid_swa_sink_mla_gated_macaron"


def rms_norm(x, g):
    xf = x.astype(jnp.float32)
    y = xf * lax.rsqrt(jnp.mean(xf * xf, axis=-1, keepdims=True) + EPS)
    return (y * g.astype(jnp.float32)).astype(x.dtype)


def swiglu(x, w_gate, w_up, w_down):
    return (jax.nn.silu(x @ w_gate) * (x @ w_up)) @ w_down


def apply_rope(x, cos, sin):
    half = x.shape[-1] // 2
    x1, x2 = x[..., :half], x[..., half:]
    return jnp.concatenate([x1 * cos - x2 * sin, x2 * cos + x1 * sin], axis=-1)


def alibi_slopes(n_heads):
    return jnp.exp2(-8.0 * (jnp.arange(n_heads, dtype=jnp.float32) + 1.0) / n_heads)


def swa_attention(q, k, v, positions, q_gain, k_gain, sinks):
    B, S = q.shape[0], q.shape[1]
    nb = S // BLOCK
    q = rms_norm(q, q_gain)
    k = rms_norm(k, k_gain)
    qb = q.reshape(B, nb, BLOCK, N_KV_HEADS_A, GROUP_A, HEAD_DIM_A)
    kb = k.reshape(B, nb, BLOCK, N_KV_HEADS_A, HEAD_DIM_A)
    vb = v.reshape(B, nb, BLOCK, N_KV_HEADS_A, HEAD_DIM_A)
    pad5 = ((0, 0), (1, 0), (0, 0), (0, 0), (0, 0))
    kk = jnp.concatenate([jnp.pad(kb, pad5)[:, :-1], kb], axis=2)
    vv = jnp.concatenate([jnp.pad(vb, pad5)[:, :-1], vb], axis=2)
    pb = positions.reshape(B, nb, BLOCK)
    pk = jnp.concatenate([jnp.pad(pb, ((0, 0), (1, 0), (0, 0)))[:, :-1], pb], axis=2)

    scale = HEAD_DIM_A ** -0.5
    s = jnp.einsum('bnqkgd,bnskd->bnkgqs', qb, kk).astype(jnp.float32) * scale
    dist = (pb[:, :, :, None] - pk[:, :, None, :]).astype(jnp.float32)
    slopes = alibi_slopes(N_HEADS_A).reshape(N_KV_HEADS_A, GROUP_A)
    s = s - slopes[None, None, :, :, None, None] * dist[:, :, None, None]

    qi = jnp.arange(BLOCK)[:, None] + BLOCK
    ki = jnp.arange(2 * BLOCK)[None, :]
    diff = qi - ki
    band = (diff >= 0) & (diff < WINDOW)
    has_prev = (jnp.arange(nb)[:, None, None] > 0) | (ki[None] >= BLOCK)
    valid = band[None] & has_prev
    s = jnp.where(valid[None, :, None, None], s, NEG)

    sink = sinks.astype(jnp.float32).reshape(N_KV_HEADS_A, GROUP_A)
    sink = jnp.broadcast_to(sink[None, None, :, :, None, None], s.shape[:-1] + (1,))
    p = jax.nn.softmax(jnp.concatenate([s, sink], axis=-1), axis=-1)[..., :-1]
    o = jnp.einsum('bnkgqs,bnskd->bnqkgd', p.astype(vv.dtype), vv)
    return o.reshape(B, S, WIDTH_A)


def mla_attention(c_q, c_kv, k_rope, cos, sin, q_lora_norm, w_uq, kv_lora_norm, w_ukv, q_gain, k_gain):
    B, S = c_q.shape[0], c_q.shape[1]
    nb = S // BLOCK
    q = (rms_norm(c_q, q_lora_norm) @ w_uq).reshape(B, S, N_HEADS_B, QK_DIM_B)
    kv = (rms_norm(c_kv, kv_lora_norm) @ w_ukv).reshape(B, S, N_HEADS_B, QK_NOPE_DIM + V_DIM_B)
    k_nope, v = kv[..., :QK_NOPE_DIM], kv[..., QK_NOPE_DIM:]
    k_r = jnp.broadcast_to(k_rope[:, :, None, :], (B, S, N_HEADS_B, QK_ROPE_DIM))
    k = jnp.concatenate([k_nope, k_r], axis=-1)
    q = rms_norm(q, q_gain)
    k = rms_norm(k, k_gain)
    q = jnp.concatenate([q[..., :QK_NOPE_DIM], apply_rope(q[..., QK_NOPE_DIM:], cos, sin)], axis=-1)
    k = jnp.concatenate([k[..., :QK_NOPE_DIM], apply_rope(k[..., QK_NOPE_DIM:], cos, sin)], axis=-1)

    scale = QK_DIM_B ** -0.5
    key_idx = jnp.arange(S)
    qb = jnp.moveaxis(q.reshape(B, nb, BLOCK, N_HEADS_B, QK_DIM_B), 1, 0)

    def one_block(args):
        q_blk, i = args
        s = jnp.einsum('bqhd,bshd->bhqs', q_blk, k).astype(jnp.float32) * scale
        t = i * BLOCK + jnp.arange(BLOCK)
        s = jnp.where(t[:, None] >= key_idx[None, :], s, NEG)
        p = jax.nn.softmax(s, axis=-1)
        return jnp.einsum('bhqs,bshd->bqhd', p.astype(v.dtype), v)

    o = lax.map(one_block, (qb, jnp.arange(nb)))
    return jnp.moveaxis(o, 0, 1).reshape(B, S, WIDTH_B)


def setup_inputs(seed: int = 0) -> dict:
    key = jax.random.key(seed)
    ks = iter(jax.random.split(key, 32))
    L = DEPTH

    def w(shape, fan_in):
        return jax.random.normal(next(ks), shape, jnp.float32) * fan_in ** -0.5

    def gain(shape):
        return 1.0 + 0.02 * jax.random.normal(next(ks), shape, jnp.float32)

    x = jax.random.normal(next(ks), (BATCH, SEQ, D_MODEL), jnp.float32)
    start = jax.random.randint(next(ks), (BATCH, 1), 0, 4096, dtype=jnp.int32)
    positions = start + jnp.arange(SEQ, dtype=jnp.int32)[None, :]
    return {
        "x": x,
        "positions": positions,
        "ffn1_norm": gain((L, D_MODEL)),
        "ffn1_w_gate": w((L, D_MODEL, D_FF), D_MODEL),
        "ffn1_w_up": w((L, D_MODEL, D_FF), D_MODEL),
        "ffn1_w_down": w((L, D_FF, D_MODEL), D_FF),
        "mix_norm": gain((L, D_MODEL)),
        "w_in": w((L, D_MODEL, IN_WIDTH), D_MODEL),
        "swa_q_norm": gain((L, HEAD_DIM_A)),
        "swa_k_norm": gain((L, HEAD_DIM_A)),
        "swa_sinks": 0.5 * jax.random.normal(next(ks), (L, N_HEADS_A), jnp.float32),
        "mla_q_lora_norm": gain((L, Q_LORA_RANK)),
        "mla_w_uq": w((L, Q_LORA_RANK, N_HEADS_B * QK_DIM_B), Q_LORA_RANK),
        "mla_kv_lora_norm": gain((L, KV_LORA_RANK)),
        "mla_w_ukv": w((L, KV_LORA_RANK, N_HEADS_B * (QK_NOPE_DIM + V_DIM_B)), KV_LORA_RANK),
        "mla_q_norm": gain((L, QK_DIM_B)),
        "mla_k_norm": gain((L, QK_DIM_B)),
        "w_branch_a": w((L, WIDTH_A, D_MODEL), WIDTH_A),
        "w_branch_b": w((L, WIDTH_B, D_MODEL), WIDTH_B),
        "w_out": w((L, D_MODEL, D_MODEL), D_MODEL),
        "ffn2_norm": gain((L, D_MODEL)),
        "ffn2_w_gate": w((L, D_MODEL, D_FF), D_MODEL),
        "ffn2_w_up": w((L, D_MODEL, D_FF), D_MODEL),
        "ffn2_w_down": w((L, D_FF, D_MODEL), D_FF),
    }


def reference(x, positions, ffn1_norm, ffn1_w_gate, ffn1_w_up, ffn1_w_down, mix_norm, w_in,
              swa_q_norm, swa_k_norm, swa_sinks, mla_q_lora_norm, mla_w_uq, mla_kv_lora_norm,
              mla_w_ukv, mla_q_norm, mla_k_norm, w_branch_a, w_branch_b, w_out,
              ffn2_norm, ffn2_w_gate, ffn2_w_up, ffn2_w_down):
    B, S, _ = x.shape
    half = QK_ROPE_DIM // 2
    inv_freq = ROPE_BASE ** (-jnp.arange(half, dtype=jnp.float32) / half)
    ang = positions.astype(jnp.float32)[..., None] * inv_freq
    cos = jnp.cos(ang)[:, :, None, :].astype(x.dtype)
    sin = jnp.sin(ang)[:, :, None, :].astype(x.dtype)
    k_rope_cos, k_rope_sin = cos[:, :, 0], sin[:, :, 0]

    for l in range(DEPTH):
        x = x + 0.5 * swiglu(rms_norm(x, ffn1_norm[l]), ffn1_w_gate[l], ffn1_w_up[l], ffn1_w_down[l])

        h = rms_norm(x, mix_norm[l])
        proj = h @ w_in[l]
        qa, ka, va, c_q, c_kv, k_r, g_a, g_b = jnp.split(proj, IN_OFFSETS, axis=-1)
        qa = qa.reshape(B, S, N_HEADS_A, HEAD_DIM_A)
        ka = ka.reshape(B, S, N_KV_HEADS_A, HEAD_DIM_A)
        va = va.reshape(B, S, N_KV_HEADS_A, HEAD_DIM_A)
        o_a = swa_attention(qa, ka, va, positions, swa_q_norm[l], swa_k_norm[l], swa_sinks[l])
        o_b = mla_attention(c_q, c_kv, k_r, cos, sin, mla_q_lora_norm[l], mla_w_uq[l],
                            mla_kv_lora_norm[l], mla_w_ukv[l], mla_q_norm[l], mla_k_norm[l])
        merged = jax.nn.sigmoid(g_a) * (o_a @ w_branch_a[l]) + jax.nn.sigmoid(g_b) * (o_b @ w_branch_b[l])
        x = x + merged @ w_out[l]

        x = x + 0.5 * swiglu(rms_norm(x, ffn2_norm[l]), ffn2_w_gate[l], ffn2_w_up[l], ffn2_w_down[l])
    return x
```

```python
import functools

import numpy as np
import jax
import jax.numpy as jnp
from jax import lax
from jax.experimental import pallas as pl
from jax.experimental.pallas import tpu as pltpu

F32 = jnp.float32
BF16 = jnp.bfloat16

D_MODEL = 1024
D_FF = 2816
HEAD_DIM_A = 64
N_HEADS_A = 8
N_KV_HEADS_A = 2
GROUP_A = N_HEADS_A // N_KV_HEADS_A
WINDOW = 128
N_HEADS_B = 8
Q_LORA_RANK = 256
KV_LORA_RANK = 128
QK_NOPE_DIM = 64
QK_ROPE_DIM = 32
QK_DIM_B = QK_NOPE_DIM + QK_ROPE_DIM
V_DIM_B = 64
ROPE_BASE = 10000.0
WIDTH_A = N_HEADS_A * HEAD_DIM_A
WIDTH_B = N_HEADS_B * V_DIM_B
EPS = 1e-6
NEG = -1e30

LANES = 128
HALF = LANES // 2

OFF_QA = 0
OFF_KA = OFF_QA + WIDTH_A
OFF_VA = OFF_KA + N_KV_HEADS_A * HEAD_DIM_A
OFF_CQ = OFF_VA + N_KV_HEADS_A * HEAD_DIM_A
OFF_CKV = OFF_CQ + Q_LORA_RANK
OFF_KR = OFF_CKV + KV_LORA_RANK
OFF_GA = OFF_KR + QK_ROPE_DIM
OFF_GB = OFF_GA + D_MODEL
IN_WIDTH = OFF_GB + D_MODEL

TOKEN_TILE = 512
ATTN_BLOCK_B = 256
VMEM_LIMIT = 56 * 1024 * 1024

_ROPE_HALF = QK_ROPE_DIM // 2
_MLA_SRC = np.full(LANES, -1, np.int64)
_MLA_SRC[0:32] = np.arange(0, 32)
_MLA_SRC[32:32 + _ROPE_HALF] = QK_NOPE_DIM + np.arange(_ROPE_HALF)
_MLA_SRC[64:96] = np.arange(32, 64)
_MLA_SRC[96:96 + _ROPE_HALF] = QK_NOPE_DIM + _ROPE_HALF + np.arange(_ROPE_HALF)
_X1_LANE0 = 32
_X2_LANE0 = 96

_SWA_HEAD_ORDER = np.array([h for j in range(GROUP_A) for h in (j, j + GROUP_A)])
_SWA_COLS = (_SWA_HEAD_ORDER[:, None] * HEAD_DIM_A + np.arange(HEAD_DIM_A)[None, :]).reshape(-1)


def _rms(x, g, n):
    ssq = jnp.sum(x * x, axis=-1, keepdims=True)
    return x * lax.rsqrt(ssq / n + EPS) * g


def _dot(a, b):
    return jnp.dot(a, b, preferred_element_type=F32)


def _dot_nt(a, b):
    return lax.dot_general(a, b, (((1,), (1,)), ((), ())), preferred_element_type=F32)


def _const_spec(shape):
    return pl.BlockSpec(shape, lambda *_: (0,) * len(shape), pipeline_mode=pl.Buffered(1))


def _params(n_axes):
    return pltpu.CompilerParams(dimension_semantics=("parallel",) * n_axes,
                                vmem_limit_bytes=VMEM_LIMIT)


def _ffn_kernel(x_ref, g_ref, wg_ref, wu_ref, wd_ref, o_ref):
    x = x_ref[...]
    h = _rms(x, g_ref[...], D_MODEL).astype(BF16)
    gate = _dot(h, wg_ref[...])
    up = _dot(h, wu_ref[...])
    act = (gate * jax.nn.sigmoid(gate) * up).astype(BF16)
    o_ref[...] = x + 0.5 * _dot(act, wd_ref[...])


def _ffn(x, g, wg, wu, wd):
    t = x.shape[0]
    tm = TOKEN_TILE
    row = pl.BlockSpec((tm, D_MODEL), lambda i: (i, 0))
    return pl.pallas_call(
        _ffn_kernel,
        grid=(t // tm,),
        in_specs=[row, _const_spec((1, D_MODEL)), _const_spec((D_MODEL, D_FF)),
                  _const_spec((D_MODEL, D_FF)), _const_spec((D_FF, D_MODEL))],
        out_specs=row,
        out_shape=jax.ShapeDtypeStruct((t, D_MODEL), F32),
        compiler_params=_params(1),
        name="ffn",
    )(x, g, wg, wu, wd)


def _rope_kernel(pos_ref, invf_ref, sign_ref, c_ref, s_ref):
    ang = pos_ref[...].astype(F32) * invf_ref[...]
    c_ref[...] = jnp.cos(ang)
    s_ref[...] = jnp.sin(ang) * sign_ref[...]


def _rope_tables(pos_col):
    t = pos_col.shape[0]
    tm = TOKEN_TILE
    inv_freq = ROPE_BASE ** (-jnp.arange(_ROPE_HALF, dtype=F32) / _ROPE_HALF)
    invf = jnp.zeros((1, LANES), F32)
    invf = invf.at[0, _X1_LANE0:_X1_LANE0 + _ROPE_HALF].set(inv_freq)
    invf = invf.at[0, _X2_LANE0:_X2_LANE0 + _ROPE_HALF].set(inv_freq)
    sign = np.zeros((1, LANES), np.float32)
    sign[0, _X1_LANE0:_X1_LANE0 + _ROPE_HALF] = -1.0
    sign[0, _X2_LANE0:_X2_LANE0 + _ROPE_HALF] = 1.0
    out = jax.ShapeDtypeStruct((t, LANES), F32)
    return pl.pallas_call(
        _rope_kernel,
        grid=(t // tm,),
        in_specs=[pl.BlockSpec((tm, 1), lambda i: (i, 0)),
                  _const_spec((1, LANES)), _const_spec((1, LANES))],
        out_specs=[pl.BlockSpec((tm, LANES), lambda i: (i, 0))] * 2,
        out_shape=[out, out],
        compiler_params=_params(1),
        name="rope_tables",
    )(pos_col, invf, jnp.asarray(sign))


def _mix_proj_kernel(x_ref, g_ref, win_ref, cqn_ref, wuq_ref, ckvn_ref, wukv_ref,
                     qag_ref, kag_ref, qbg_ref, kbg_ref, cos_ref, sin_ref,
                     qa_ref, ka_ref, va_ref, qb_ref, kb_ref, vb_ref):
    x = x_ref[...]
    h = _rms(x, g_ref[...], D_MODEL).astype(BF16)
    proj = _dot(h, win_ref[...])
    lane = lax.broadcasted_iota(jnp.int32, (1, LANES), 1)
    lo = lane < HALF

    def half_norm(c, gain):
        sq = c * c
        s_lo = jnp.sum(jnp.where(lo, sq, 0.0), axis=-1, keepdims=True)
        s_hi = jnp.sum(jnp.where(lo, 0.0, sq), axis=-1, keepdims=True)
        inv = jnp.where(lo, lax.rsqrt(s_lo / HEAD_DIM_A + EPS), lax.rsqrt(s_hi / HEAD_DIM_A + EPS))
        return c * inv * gain

    for j in range(WIDTH_A // LANES):
        c = proj[:, OFF_QA + j * LANES: OFF_QA + (j + 1) * LANES]
        qa_ref[:, j * LANES:(j + 1) * LANES] = half_norm(c, qag_ref[...]).astype(BF16)
    ka_ref[...] = half_norm(proj[:, 512:640], kag_ref[...]).astype(BF16)
    va_ref[...] = proj[:, 640:768].astype(BF16)

    cq = _rms(proj[:, 768:1024], cqn_ref[...], Q_LORA_RANK).astype(BF16)
    ckv = _rms(proj[:, 1024:1152], ckvn_ref[...], KV_LORA_RANK).astype(BF16)
    kr = proj[:, 1152:1280]
    qup = _dot(cq, wuq_ref[...])
    kvup = _dot(ckv, wukv_ref[...])
    cos = cos_ref[...]
    sin = sin_ref[...]

    def norm_rope(c, gain):
        y = _rms(c, gain, QK_DIM_B)
        return (y * cos + pltpu.roll(y, HALF, 1) * sin).astype(BF16)

    for hd in range(N_HEADS_B):
        sl = slice(hd * LANES, (hd + 1) * LANES)
        qb_ref[:, sl] = norm_rope(qup[:, sl], qbg_ref[...])
        kb_ref[:, sl] = norm_rope(kvup[:, sl] + kr, kbg_ref[...])
    vb_ref[...] = kvup[:, N_HEADS_B * LANES:].astype(BF16)


_PROJ_W = WIDTH_A + 2 * LANES + Q_LORA_RANK + KV_LORA_RANK + LANES


def _mix_proj(x, g, win_p, cqn, wuq_p, ckvn, wukv_p, qag, kag, qbg, kbg, cos_t, sin_t):
    t = x.shape[0]
    tm = TOKEN_TILE

    def row(w):
        return pl.BlockSpec((tm, w), lambda i: (i, 0))

    widths = (WIDTH_A, LANES, LANES, N_HEADS_B * LANES, N_HEADS_B * LANES, WIDTH_B)
    return pl.pallas_call(
        _mix_proj_kernel,
        grid=(t // tm,),
        in_specs=[row(D_MODEL), _const_spec((1, D_MODEL)), _const_spec((D_MODEL, _PROJ_W)),
                  _const_spec((1, Q_LORA_RANK)), _const_spec((Q_LORA_RANK, N_HEADS_B * LANES)),
                  _const_spec((1, KV_LORA_RANK)),
                  _const_spec((KV_LORA_RANK, N_HEADS_B * LANES + WIDTH_B)),
                  _const_spec((1, LANES)), _const_spec((1, LANES)),
                  _const_spec((1, LANES)), _const_spec((1, LANES)),
                  row(LANES), row(LANES)],
        out_specs=[row(w) for w in widths],
        out_shape=[jax.ShapeDtypeStruct((t, w), BF16) for w in widths],
        compiler_params=_params(1),
        name="mix_proj",
    )(x, g, win_p, cqn, wuq_p, ckvn, wukv_p, qag, kag, qbg, kbg, cos_t, sin_t)


def _swa_kernel(sink_ref, q_ref, kp_ref, kc_ref, vp_ref, vc_ref, pq_ref, pkp_ref, pkc_ref, o_ref):
    i = pl.program_id(1)
    blk = WINDOW
    lane = lax.broadcasted_iota(jnp.int32, (1, LANES), 1)
    lo = lane < HALF
    kk = jnp.concatenate([kp_ref[0], kc_ref[0]], axis=0)
    vv = jnp.concatenate([vp_ref[0], vc_ref[0]], axis=0)
    pk = jnp.concatenate([pkp_ref[0], pkc_ref[0]], axis=1)
    dist = (pq_ref[0] - pk).astype(F32)
    qi = lax.broadcasted_iota(jnp.int32, (blk, 2 * blk), 0) + blk
    ki = lax.broadcasted_iota(jnp.int32, (blk, 2 * blk), 1)
    diff = qi - ki
    valid = (diff >= 0) & (diff < WINDOW) & ((ki >= blk) | (i > 0))
    zero = jnp.zeros((), BF16)

    outs = [None] * GROUP_A
    for kv in range(N_KV_HEADS_A):
        keep = lo if kv == 0 else jnp.logical_not(lo)
        qs = [jnp.where(keep, q_ref[0, :, j * LANES:(j + 1) * LANES], zero) for j in range(GROUP_A)]
        s_all = _dot_nt(jnp.concatenate(qs, axis=0), kk)
        ps, inv_ls = [], []
        for j in range(GROUP_A):
            hd = kv * GROUP_A + j
            slope = 2.0 ** (-8.0 * (hd + 1) / N_HEADS_A)
            sink = sink_ref[hd]
            s = s_all[j * blk:(j + 1) * blk] - slope * dist
            s = jnp.where(valid, s, NEG)
            m = jnp.maximum(jnp.max(s, axis=-1, keepdims=True), sink)
            p = jnp.exp(s - m)
            l = jnp.sum(p, axis=-1, keepdims=True) + jnp.exp(sink - m)
            ps.append(p.astype(BF16))
            inv_ls.append(1.0 / l)
        o_all = _dot(jnp.concatenate(ps, axis=0), vv)
        for j in range(GROUP_A):
            o = o_all[j * blk:(j + 1) * blk] * inv_ls[j]
            outs[j] = o if kv == 0 else jnp.where(lo, outs[j], o)
    for j in range(GROUP_A):
        o_ref[0, :, j * LANES:(j + 1) * LANES] = outs[j].astype(BF16)


def _swa_attn(sinks, qa, ka, va, pos_col, pos_row):
    b, s, _ = qa.shape
    blk = WINDOW
    cur = lambda bi, i: (bi, i, 0)
    prev = lambda bi, i: (bi, jnp.maximum(i - 1, 0), 0)
    return pl.pallas_call(
        _swa_kernel,
        grid=(b, s // blk),
        in_specs=[pl.BlockSpec(memory_space=pltpu.SMEM),
                  pl.BlockSpec((1, blk, WIDTH_A), cur),
                  pl.BlockSpec((1, blk, LANES), prev), pl.BlockSpec((1, blk, LANES), cur),
                  pl.BlockSpec((1, blk, LANES), prev), pl.BlockSpec((1, blk, LANES), cur),
                  pl.BlockSpec((1, blk, 1), cur),
                  pl.BlockSpec((1, 1, blk), lambda bi, i: (bi, 0, jnp.maximum(i - 1, 0))),
                  pl.BlockSpec((1, 1, blk), lambda bi, i: (bi, 0, i))],
        out_specs=pl.BlockSpec((1, blk, WIDTH_A), cur),
        out_shape=jax.ShapeDtypeStruct((b, s, WIDTH_A), BF16),
        compiler_params=_params(2),
        name="swa_attn",
    )(sinks, qa, ka, ka, va, va, pos_col, pos_row, pos_row)


def _mla_kernel(q_ref, k_ref, v_ref, o_ref):
    i = pl.program_id(1)
    tq = ATTN_BLOCK_B
    scale = QK_DIM_B ** -0.5
    lane = lax.broadcasted_iota(jnp.int32, (1, LANES), 1)
    lo = lane < HALF
    row = lax.broadcasted_iota(jnp.int32, (tq, tq), 0)
    col = lax.broadcasted_iota(jnp.int32, (tq, tq), 1)
    causal = row >= col
    zero = jnp.zeros((), BF16)

    for pair in range(N_HEADS_B // 2):
        out = None
        for half in range(2):
            hd = 2 * pair + half
            hsl = slice(hd * LANES, (hd + 1) * LANES)
            psl = slice(pair * LANES, (pair + 1) * LANES)
            keep = lo if half == 0 else jnp.logical_not(lo)
            q = q_ref[0, :, hsl]

            def step(kb, carry, masked):
                m, l, acc = carry
                ks = pl.multiple_of(kb * tq, tq)
                k = k_ref[0, pl.ds(ks, tq), hsl]
                v = jnp.where(keep, v_ref[0, pl.ds(ks, tq), psl], zero)
                s = _dot_nt(q, k) * scale
                if masked:
                    s = jnp.where(causal, s, NEG)
                m_new = jnp.maximum(m, jnp.max(s, axis=-1, keepdims=True))
                alpha = jnp.exp(m - m_new)
                p = jnp.exp(s - m_new)
                l = alpha * l + jnp.sum(p, axis=-1, keepdims=True)
                acc = alpha * acc + _dot(p.astype(BF16), v)
                return m_new, l, acc

            init = (jnp.full((tq, 1), NEG, F32), jnp.zeros((tq, 1), F32),
                    jnp.zeros((tq, LANES), F32))
            carry = lax.fori_loop(0, i, functools.partial(step, masked=False), init)
            _, l, acc = step(i, carry, masked=True)
            o = acc / l
            out = o if half == 0 else out + o
        o_ref[0, :, pair * LANES:(pair + 1) * LANES] = out.astype(BF16)


def _mla_attn(qb, kb, vb):
    b, s, _ = qb.shape
    tq = ATTN_BLOCK_B
    return pl.pallas_call(
        _mla_kernel,
        grid=(b, s // tq),
        in_specs=[pl.BlockSpec((1, tq, N_HEADS_B * LANES), lambda bi, i: (bi, i, 0)),
                  pl.BlockSpec((1, s, N_HEADS_B * LANES), lambda bi, i: (bi, 0, 0)),
                  pl.BlockSpec((1, s, WIDTH_B), lambda bi, i: (bi, 0, 0))],
        out_specs=pl.BlockSpec((1, tq, WIDTH_B), lambda bi, i: (bi, i, 0)),
        out_shape=jax.ShapeDtypeStruct((b, s, WIDTH_B), BF16),
        compiler_params=_params(2),
        name="mla_attn",
    )(qb, kb, vb)


def _merge_kernel(x_ref, g_ref, oa_ref, ob_ref, wg_ref, wba_ref, wbb_ref, wo_ref, o_ref):
    x = x_ref[...]
    h = _rms(x, g_ref[...], D_MODEL).astype(BF16)
    gates = _dot(h, wg_ref[...])
    ma = _dot(oa_ref[...], wba_ref[...])
    mb = _dot(ob_ref[...], wbb_ref[...])
    merged = jax.nn.sigmoid(gates[:, :D_MODEL]) * ma + jax.nn.sigmoid(gates[:, D_MODEL:]) * mb
    o_ref[...] = x + _dot(merged.astype(BF16), wo_ref[...])


def _merge(x, g, oa, ob, wg, wba, wbb, wo):
    t = x.shape[0]
    tm = TOKEN_TILE

    def row(w):
        return pl.BlockSpec((tm, w), lambda i: (i, 0))

    return pl.pallas_call(
        _merge_kernel,
        grid=(t // tm,),
        in_specs=[row(D_MODEL), _const_spec((1, D_MODEL)), row(WIDTH_A), row(WIDTH_B),
                  _const_spec((D_MODEL, 2 * D_MODEL)), _const_spec((WIDTH_A, D_MODEL)),
                  _const_spec((WIDTH_B, D_MODEL)), _const_spec((D_MODEL, D_MODEL))],
        out_specs=row(D_MODEL),
        out_shape=jax.ShapeDtypeStruct((t, D_MODEL), F32),
        compiler_params=_params(1),
        name="merge",
    )(x, g, oa, ob, wg, wba, wbb, wo)


def _mla_slot_cols(w, per_head, col0):
    src = np.where(_MLA_SRC >= 0, _MLA_SRC, 0)
    cols = (np.arange(N_HEADS_B)[:, None] * per_head + col0 + src[None, :]).reshape(-1)
    mask = np.tile(_MLA_SRC >= 0, N_HEADS_B)
    return jnp.where(mask[None, :], jnp.take(w, cols, axis=1), 0.0)


def _mla_slot_vec(v):
    src = np.where(_MLA_SRC >= 0, _MLA_SRC, 0)
    return jnp.where(_MLA_SRC >= 0, jnp.take(v, src), 0.0).reshape(1, LANES)


def kernel(x, positions, ffn1_norm, ffn1_w_gate, ffn1_w_up, ffn1_w_down, mix_norm, w_in, swa_q_norm, swa_k_norm, swa_sinks, mla_q_lora_norm, mla_w_uq, mla_kv_lora_norm, mla_w_ukv, mla_q_norm, mla_k_norm, w_branch_a, w_branch_b, w_out, ffn2_norm, ffn2_w_gate, ffn2_w_up, ffn2_w_down):
    b, s, d = x.shape
    t = b * s
    depth = w_in.shape[0]
    assert d == D_MODEL and s % ATTN_BLOCK_B == 0 and t % TOKEN_TILE == 0

    pos_col = positions.reshape(t, 1)
    cos_t, sin_t = _rope_tables(pos_col)
    pos_col3 = positions.reshape(b, s, 1)
    pos_row3 = positions.reshape(b, 1, s)

    kr_cols = np.where((_MLA_SRC >= QK_NOPE_DIM), _MLA_SRC - QK_NOPE_DIM, 0)
    kr_mask = _MLA_SRC >= QK_NOPE_DIM
    kn_mask = (_MLA_SRC >= 0) & (_MLA_SRC < QK_NOPE_DIM)
    kn_cols = (np.arange(N_HEADS_B)[:, None] * (QK_NOPE_DIM + V_DIM_B)
               + np.where(kn_mask, _MLA_SRC, 0)[None, :]).reshape(-1)
    v_cols = (np.arange(N_HEADS_B)[:, None] * (QK_NOPE_DIM + V_DIM_B) + QK_NOPE_DIM
              + np.arange(V_DIM_B)[None, :]).reshape(-1)

    xt = x.reshape(t, d)
    for l in range(depth):
        xt = _ffn(xt, ffn1_norm[l].reshape(1, d), ffn1_w_gate[l].astype(BF16),
                  ffn1_w_up[l].astype(BF16), ffn1_w_down[l].astype(BF16))

        wi = w_in[l]
        kr_w = jnp.where(kr_mask[None, :], jnp.take(wi, OFF_KR + kr_cols, axis=1), 0.0)
        win_p = jnp.concatenate(
            [jnp.take(wi, OFF_QA + _SWA_COLS, axis=1), wi[:, OFF_KA:OFF_KR], kr_w], axis=1).astype(BF16)
        wuq_p = _mla_slot_cols(mla_w_uq[l], QK_DIM_B, 0).astype(BF16)
        wuk = jnp.where(np.tile(kn_mask, N_HEADS_B)[None, :], jnp.take(mla_w_ukv[l], kn_cols, axis=1), 0.0)
        wukv_p = jnp.concatenate([wuk, jnp.take(mla_w_ukv[l], v_cols, axis=1)], axis=1).astype(BF16)
        qag = (jnp.tile(swa_q_norm[l], 2) * HEAD_DIM_A ** -0.5).reshape(1, LANES)
        kag = jnp.tile(swa_k_norm[l], 2).reshape(1, LANES)
        qa, ka, va, qb, kb, vb = _mix_proj(
            xt, mix_norm[l].reshape(1, d), win_p, mla_q_lora_norm[l].reshape(1, -1), wuq_p,
            mla_kv_lora_norm[l].reshape(1, -1), wukv_p, qag, kag,
            _mla_slot_vec(mla_q_norm[l]), _mla_slot_vec(mla_k_norm[l]), cos_t, sin_t)

        oa = _swa_attn(swa_sinks[l], qa.reshape(b, s, -1), ka.reshape(b, s, -1), va.reshape(b, s, -1),
                       pos_col3, pos_row3)
        ob = _mla_attn(qb.reshape(b, s, -1), kb.reshape(b, s, -1), vb.reshape(b, s, -1))

        xt = _merge(xt, mix_norm[l].reshape(1, d), oa.reshape(t, -1), ob.reshape(t, -1),
                    wi[:, OFF_GA:].astype(BF16), jnp.take(w_branch_a[l], _SWA_COLS, axis=0).astype(BF16),
                    w_branch_b[l].astype(BF16), w_out[l].astype(BF16))

        xt = _ffn(xt, ffn2_norm[l].reshape(1, d), ffn2_w_gate[l].astype(BF16),
                  ffn2_w_up[l].astype(BF16), ffn2_w_down[l].astype(BF16))
    return xt.reshape(b, s, d)
```

```python
import functools

import numpy as np
import jax
import jax.numpy as jnp
from jax import lax
from jax.experimental import pallas as pl
from jax.experimental.pallas import tpu as pltpu

F32 = jnp.float32
BF16 = jnp.bfloat16

D_MODEL = 1024
D_FF = 2816
HEAD_DIM_A = 64
N_HEADS_A = 8
N_KV_HEADS_A = 2
GROUP_A = N_HEADS_A // N_KV_HEADS_A
WINDOW = 128
N_HEADS_B = 8
Q_LORA_RANK = 256
KV_LORA_RANK = 128
QK_NOPE_DIM = 64
QK_ROPE_DIM = 32
QK_DIM_B = QK_NOPE_DIM + QK_ROPE_DIM
V_DIM_B = 64
ROPE_BASE = 10000.0
WIDTH_A = N_HEADS_A * HEAD_DIM_A
WIDTH_B = N_HEADS_B * V_DIM_B
EPS = 1e-6
NEG = -1e30

LANES = 128
HALF = LANES // 2

OFF_QA = 0
OFF_KA = OFF_QA + WIDTH_A
OFF_VA = OFF_KA + N_KV_HEADS_A * HEAD_DIM_A
OFF_CQ = OFF_VA + N_KV_HEADS_A * HEAD_DIM_A
OFF_CKV = OFF_CQ + Q_LORA_RANK
OFF_KR = OFF_CKV + KV_LORA_RANK
OFF_GA = OFF_KR + QK_ROPE_DIM
OFF_GB = OFF_GA + D_MODEL
IN_WIDTH = OFF_GB + D_MODEL

TOKEN_TILE = 512
ATTN_BLOCK_B = 256
VMEM_LIMIT = 56 * 1024 * 1024

_ROPE_HALF = QK_ROPE_DIM // 2
_MLA_SRC = np.full(LANES, -1, np.int64)
_MLA_SRC[0:32] = np.arange(0, 32)
_MLA_SRC[32:32 + _ROPE_HALF] = QK_NOPE_DIM + np.arange(_ROPE_HALF)
_MLA_SRC[64:96] = np.arange(32, 64)
_MLA_SRC[96:96 + _ROPE_HALF] = QK_NOPE_DIM + _ROPE_HALF + np.arange(_ROPE_HALF)
_X1_LANE0 = 32
_X2_LANE0 = 96

_SWA_HEAD_ORDER = np.array([h for j in range(GROUP_A) for h in (j, j + GROUP_A)])
_SWA_COLS = (_SWA_HEAD_ORDER[:, None] * HEAD_DIM_A + np.arange(HEAD_DIM_A)[None, :]).reshape(-1)


def _rms(x, g, n):
    ssq = jnp.sum(x * x, axis=-1, keepdims=True)
    return x * lax.rsqrt(ssq / n + EPS) * g


def _dot(a, b):
    return jnp.dot(a, b, preferred_element_type=F32)


def _dot_nt(a, b):
    return lax.dot_general(a, b, (((1,), (1,)), ((), ())), preferred_element_type=F32)


def _const_spec(shape):
    return pl.BlockSpec(shape, lambda *_: (0,) * len(shape), pipeline_mode=pl.Buffered(1))


def _params(n_axes):
    return pltpu.CompilerParams(dimension_semantics=("parallel",) * n_axes,
                                vmem_limit_bytes=VMEM_LIMIT)


def _ffn_kernel(x_ref, g_ref, wg_ref, wu_ref, wd_ref, o_ref):
    x = x_ref[...]
    h = _rms(x, g_ref[...], D_MODEL).astype(BF16)
    gate = _dot(h, wg_ref[...])
    up = _dot(h, wu_ref[...])
    act = (gate * jax.nn.sigmoid(gate) * up).astype(BF16)
    o_ref[...] = x + 0.5 * _dot(act, wd_ref[...])


def _ffn(x, g, wg, wu, wd):
    t = x.shape[0]
    tm = TOKEN_TILE
    row = pl.BlockSpec((tm, D_MODEL), lambda i: (i, 0))
    return pl.pallas_call(
        _ffn_kernel,
        grid=(t // tm,),
        in_specs=[row, _const_spec((1, D_MODEL)), _const_spec((D_MODEL, D_FF)),
                  _const_spec((D_MODEL, D_FF)), _const_spec((D_FF, D_MODEL))],
        out_specs=row,
        out_shape=jax.ShapeDtypeStruct((t, D_MODEL), F32),
        compiler_params=_params(1),
        name="ffn",
    )(x, g, wg, wu, wd)


def _rope_kernel(pos_ref, invf_ref, sign_ref, c_ref, s_ref):
    ang = pos_ref[...].astype(F32) * invf_ref[...]
    c_ref[...] = jnp.cos(ang)
    s_ref[...] = jnp.sin(ang) * sign_ref[...]


def _rope_tables(pos_col):
    t = pos_col.shape[0]
    tm = TOKEN_TILE
    inv_freq = ROPE_BASE ** (-jnp.arange(_ROPE_HALF, dtype=F32) / _ROPE_HALF)
    invf = jnp.zeros((1, LANES), F32)
    invf = invf.at[0, _X1_LANE0:_X1_LANE0 + _ROPE_HALF].set(inv_freq)
    invf = invf.at[0, _X2_LANE0:_X2_LANE0 + _ROPE_HALF].set(inv_freq)
    sign = np.zeros((1, LANES), np.float32)
    sign[0, _X1_LANE0:_X1_LANE0 + _ROPE_HALF] = -1.0
    sign[0, _X2_LANE0:_X2_LANE0 + _ROPE_HALF] = 1.0
    out = jax.ShapeDtypeStruct((t, LANES), F32)
    return pl.pallas_call(
        _rope_kernel,
        grid=(t // tm,),
        in_specs=[pl.BlockSpec((tm, 1), lambda i: (i, 0)),
                  _const_spec((1, LANES)), _const_spec((1, LANES))],
        out_specs=[pl.BlockSpec((tm, LANES), lambda i: (i, 0))] * 2,
        out_shape=[out, out],
        compiler_params=_params(1),
        name="rope_tables",
    )(pos_col, invf, jnp.asarray(sign))


def _mix_proj_kernel(x_ref, g_ref, win_ref, cqn_ref, wuq_ref, ckvn_ref, wukv_ref,
                     qag_ref, kag_ref, qbg_ref, kbg_ref, cos_ref, sin_ref,
                     qa_ref, ka_ref, va_ref, qb_ref, kb_ref, vb_ref):
    x = x_ref[...]
    h = _rms(x, g_ref[...], D_MODEL).astype(BF16)
    proj = _dot(h, win_ref[...])
    lane = lax.broadcasted_iota(jnp.int32, (1, LANES), 1)
    lo = lane < HALF

    def half_norm(c, gain):
        sq = c * c
        s_lo = jnp.sum(jnp.where(lo, sq, 0.0), axis=-1, keepdims=True)
        s_hi = jnp.sum(jnp.where(lo, 0.0, sq), axis=-1, keepdims=True)
        inv = jnp.where(lo, lax.rsqrt(s_lo / HEAD_DIM_A + EPS), lax.rsqrt(s_hi / HEAD_DIM_A + EPS))
        return c * inv * gain

    for j in range(WIDTH_A // LANES):
        c = proj[:, OFF_QA + j * LANES: OFF_QA + (j + 1) * LANES]
        qa_ref[:, j * LANES:(j + 1) * LANES] = half_norm(c, qag_ref[...]).astype(BF16)
    ka_ref[...] = half_norm(proj[:, 512:640], kag_ref[...]).astype(BF16)
    va_ref[...] = proj[:, 640:768].astype(BF16)

    cq = _rms(proj[:, 768:1024], cqn_ref[...], Q_LORA_RANK).astype(BF16)
    ckv = _rms(proj[:, 1024:1152], ckvn_ref[...], KV_LORA_RANK).astype(BF16)
    kr = proj[:, 1152:1280]
    qup = _dot(cq, wuq_ref[...])
    kvup = _dot(ckv, wukv_ref[...])
    cos = cos_ref[...]
    sin = sin_ref[...]

    def norm_rope(c, gain):
        y = _rms(c, gain, QK_DIM_B)
        return (y * cos + pltpu.roll(y, HALF, 1) * sin).astype(BF16)

    for hd in range(N_HEADS_B):
        sl = slice(hd * LANES, (hd + 1) * LANES)
        qb_ref[hd] = norm_rope(qup[:, sl], qbg_ref[...])
        kb_ref[hd] = norm_rope(kvup[:, sl] + kr, kbg_ref[...])
    vb_ref[...] = kvup[:, N_HEADS_B * LANES:].astype(BF16)


_PROJ_W = WIDTH_A + 2 * LANES + Q_LORA_RANK + KV_LORA_RANK + LANES


def _mix_proj(x, g, win_p, cqn, wuq_p, ckvn, wukv_p, qag, kag, qbg, kbg, cos_t, sin_t):
    t = x.shape[0]
    tm = TOKEN_TILE

    def row(w):
        return pl.BlockSpec((tm, w), lambda i: (i, 0))

    heads = pl.BlockSpec((N_HEADS_B, tm, LANES), lambda i: (0, i, 0))
    heads_shape = jax.ShapeDtypeStruct((N_HEADS_B, t, LANES), BF16)
    return pl.pallas_call(
        _mix_proj_kernel,
        grid=(t // tm,),
        in_specs=[row(D_MODEL), _const_spec((1, D_MODEL)), _const_spec((D_MODEL, _PROJ_W)),
                  _const_spec((1, Q_LORA_RANK)), _const_spec((Q_LORA_RANK, N_HEADS_B * LANES)),
                  _const_spec((1, KV_LORA_RANK)),
                  _const_spec((KV_LORA_RANK, N_HEADS_B * LANES + WIDTH_B)),
                  _const_spec((1, LANES)), _const_spec((1, LANES)),
                  _const_spec((1, LANES)), _const_spec((1, LANES)),
                  row(LANES), row(LANES)],
        out_specs=[row(WIDTH_A), row(LANES), row(LANES), heads, heads, row(WIDTH_B)],
        out_shape=[jax.ShapeDtypeStruct((t, w), BF16) for w in (WIDTH_A, LANES, LANES)]
        + [heads_shape, heads_shape, jax.ShapeDtypeStruct((t, WIDTH_B), BF16)],
        compiler_params=_params(1),
        name="mix_proj",
    )(x, g, win_p, cqn, wuq_p, ckvn, wukv_p, qag, kag, qbg, kbg, cos_t, sin_t)


SWA_BLOCKS_PER_STEP = 4


def _swa_kernel(sink_ref, q_ref, k_ref, v_ref, pq_ref, pk_ref, o_ref):
    step = pl.program_id(1)
    blk = WINDOW
    lane = lax.broadcasted_iota(jnp.int32, (1, LANES), 1)
    lo = lane < HALF
    qi = lax.broadcasted_iota(jnp.int32, (blk, 2 * blk), 0) + blk
    ki = lax.broadcasted_iota(jnp.int32, (blk, 2 * blk), 1)
    diff = qi - ki
    band = (diff >= 0) & (diff < WINDOW)
    zero = jnp.zeros((), BF16)

    for r in range(SWA_BLOCKS_PER_STEP):
        cur = step * SWA_BLOCKS_PER_STEP + r
        rows = slice(r * blk, (r + 1) * blk)
        if r == 0:
            prev = jnp.maximum(cur - 1, 0)
            p0 = pl.multiple_of(prev * blk, blk)
            c0 = pl.multiple_of(cur * blk, blk)
            kk = jnp.concatenate([k_ref[0, pl.ds(p0, blk), :], k_ref[0, pl.ds(c0, blk), :]], axis=0)
            vv = jnp.concatenate([v_ref[0, pl.ds(p0, blk), :], v_ref[0, pl.ds(c0, blk), :]], axis=0)
            valid = band & ((ki >= blk) | (step > 0))
        else:
            prev = cur - 1
            p0 = pl.multiple_of(prev * blk, blk)
            kk = k_ref[0, pl.ds(p0, 2 * blk), :]
            vv = v_ref[0, pl.ds(p0, 2 * blk), :]
            valid = band
        pk = jnp.concatenate([pk_ref[0, prev], pk_ref[0, cur]], axis=1)
        dist = (pq_ref[0, rows, :] - pk).astype(F32)

        outs = [None] * GROUP_A
        for kv in range(N_KV_HEADS_A):
            keep = lo if kv == 0 else jnp.logical_not(lo)
            qs = [jnp.where(keep, q_ref[0, rows, j * LANES:(j + 1) * LANES], zero) for j in range(GROUP_A)]
            s_all = _dot_nt(jnp.concatenate(qs, axis=0), kk)
            ps, inv_ls = [], []
            for j in range(GROUP_A):
                hd = kv * GROUP_A + j
                slope = 2.0 ** (-8.0 * (hd + 1) / N_HEADS_A)
                sink = sink_ref[hd]
                s = s_all[j * blk:(j + 1) * blk] - slope * dist
                s = jnp.where(valid, s, NEG)
                m = jnp.maximum(jnp.max(s, axis=-1, keepdims=True), sink)
                p = jnp.exp(s - m)
                l = jnp.sum(p, axis=-1, keepdims=True) + jnp.exp(sink - m)
                ps.append(p.astype(BF16))
                inv_ls.append(1.0 / l)
            o_all = _dot(jnp.concatenate(ps, axis=0), vv)
            for j in range(GROUP_A):
                o = o_all[j * blk:(j + 1) * blk] * inv_ls[j]
                outs[j] = o if kv == 0 else jnp.where(lo, outs[j], o)
        for j in range(GROUP_A):
            o_ref[0, rows, j * LANES:(j + 1) * LANES] = outs[j].astype(BF16)


def _swa_attn(sinks, qa, ka, va, pos_col, pos_blocks):
    b, s, _ = qa.shape
    rows = SWA_BLOCKS_PER_STEP * WINDOW
    cur = lambda bi, i: (bi, i, 0)
    whole = lambda bi, i: (bi, 0, 0)
    return pl.pallas_call(
        _swa_kernel,
        grid=(b, s // rows),
        in_specs=[pl.BlockSpec(memory_space=pltpu.SMEM),
                  pl.BlockSpec((1, rows, WIDTH_A), cur),
                  pl.BlockSpec((1, s, LANES), whole), pl.BlockSpec((1, s, LANES), whole),
                  pl.BlockSpec((1, rows, 1), cur),
                  pl.BlockSpec((1, s // WINDOW, 1, WINDOW), lambda bi, i: (bi, 0, 0, 0))],
        out_specs=pl.BlockSpec((1, rows, WIDTH_A), cur),
        out_shape=jax.ShapeDtypeStruct((b, s, WIDTH_A), BF16),
        compiler_params=_params(2),
        name="swa_attn",
    )(sinks, qa, ka, va, pos_col, pos_blocks)


def _mla_kernel(q_ref, k_ref, v_ref, o_ref):
    s_len = q_ref.shape[2]
    tq = ATTN_BLOCK_B
    c = QK_DIM_B ** -0.5 * np.log2(np.e)
    lane = lax.broadcasted_iota(jnp.int32, (1, LANES), 1)
    lo = lane < HALF
    causal = (lax.broadcasted_iota(jnp.int32, (tq, tq), 0)
              >= lax.broadcasted_iota(jnp.int32, (tq, tq), 1))
    zero = jnp.zeros((), BF16)
    v = v_ref[0]
    v_half = (jnp.where(lo, v, zero), jnp.where(lo, zero, v))

    for i in range(s_len // tq):
        n = (i + 1) * tq
        out = None
        for half in range(2):
            q = q_ref[half, 0, i * tq:n, :]
            s = _dot_nt(q, k_ref[half, 0, 0:n, :])
            diag = jnp.where(causal, s[:, n - tq:], NEG)
            s = diag if i == 0 else jnp.concatenate([s[:, :n - tq], diag], axis=1)
            m = jnp.max(s, axis=-1, keepdims=True)
            p = jnp.exp2((s - m) * c)
            l = jnp.sum(p, axis=-1, keepdims=True)
            o = _dot(p.astype(BF16), v_half[half][0:n]) / l
            out = o if half == 0 else out + o
        o_ref[0, i * tq:n, :] = out.astype(BF16)


def _mla_attn(qb, kb, vb):
    _, b, s, _ = qb.shape
    qk_spec = pl.BlockSpec((2, 1, s, LANES), lambda bi, j: (j, bi, 0, 0))
    vo_spec = pl.BlockSpec((1, s, LANES), lambda bi, j: (bi, 0, j))
    return pl.pallas_call(
        _mla_kernel,
        grid=(b, N_HEADS_B // 2),
        in_specs=[qk_spec, qk_spec, vo_spec],
        out_specs=vo_spec,
        out_shape=jax.ShapeDtypeStruct((b, s, WIDTH_B), BF16),
        compiler_params=_params(2),
        name="mla_attn",
    )(qb, kb, vb)


def _merge_kernel(x_ref, g_ref, oa_ref, ob_ref, wg_ref, wba_ref, wbb_ref, wo_ref, o_ref):
    x = x_ref[...]
    h = _rms(x, g_ref[...], D_MODEL).astype(BF16)
    gates = _dot(h, wg_ref[...])
    ma = _dot(oa_ref[...], wba_ref[...])
    mb = _dot(ob_ref[...], wbb_ref[...])
    merged = jax.nn.sigmoid(gates[:, :D_MODEL]) * ma + jax.nn.sigmoid(gates[:, D_MODEL:]) * mb
    o_ref[...] = x + _dot(merged.astype(BF16), wo_ref[...])


def _merge(x, g, oa, ob, wg, wba, wbb, wo):
    t = x.shape[0]
    tm = TOKEN_TILE

    def row(w):
        return pl.BlockSpec((tm, w), lambda i: (i, 0))

    return pl.pallas_call(
        _merge_kernel,
        grid=(t // tm,),
        in_specs=[row(D_MODEL), _const_spec((1, D_MODEL)), row(WIDTH_A), row(WIDTH_B),
                  _const_spec((D_MODEL, 2 * D_MODEL)), _const_spec((WIDTH_A, D_MODEL)),
                  _const_spec((WIDTH_B, D_MODEL)), _const_spec((D_MODEL, D_MODEL))],
        out_specs=row(D_MODEL),
        out_shape=jax.ShapeDtypeStruct((t, D_MODEL), F32),
        compiler_params=_params(1),
        name="merge",
    )(x, g, oa, ob, wg, wba, wbb, wo)


def _mla_slot_cols(w, per_head, col0):
    src = np.where(_MLA_SRC >= 0, _MLA_SRC, 0)
    cols = (np.arange(N_HEADS_B)[:, None] * per_head + col0 + src[None, :]).reshape(-1)
    mask = np.tile(_MLA_SRC >= 0, N_HEADS_B)
    return jnp.where(mask[None, :], jnp.take(w, cols, axis=1), 0.0)


def _mla_slot_vec(v):
    src = np.where(_MLA_SRC >= 0, _MLA_SRC, 0)
    return jnp.where(_MLA_SRC >= 0, jnp.take(v, src), 0.0).reshape(1, LANES)


def kernel(x, positions, ffn1_norm, ffn1_w_gate, ffn1_w_up, ffn1_w_down, mix_norm, w_in, swa_q_norm, swa_k_norm, swa_sinks, mla_q_lora_norm, mla_w_uq, mla_kv_lora_norm, mla_w_ukv, mla_q_norm, mla_k_norm, w_branch_a, w_branch_b, w_out, ffn2_norm, ffn2_w_gate, ffn2_w_up, ffn2_w_down):
    b, s, d = x.shape
    t = b * s
    depth = w_in.shape[0]
    assert d == D_MODEL and s % ATTN_BLOCK_B == 0 and t % TOKEN_TILE == 0
    assert s % (SWA_BLOCKS_PER_STEP * WINDOW) == 0

    pos_col = positions.reshape(t, 1)
    cos_t, sin_t = _rope_tables(pos_col)
    pos_col3 = positions.reshape(b, s, 1)
    pos_blocks = positions.reshape(b, s // WINDOW, 1, WINDOW)

    kr_cols = np.where((_MLA_SRC >= QK_NOPE_DIM), _MLA_SRC - QK_NOPE_DIM, 0)
    kr_mask = _MLA_SRC >= QK_NOPE_DIM
    kn_mask = (_MLA_SRC >= 0) & (_MLA_SRC < QK_NOPE_DIM)
    kn_cols = (np.arange(N_HEADS_B)[:, None] * (QK_NOPE_DIM + V_DIM_B)
               + np.where(kn_mask, _MLA_SRC, 0)[None, :]).reshape(-1)
    v_cols = (np.arange(N_HEADS_B)[:, None] * (QK_NOPE_DIM + V_DIM_B) + QK_NOPE_DIM
              + np.arange(V_DIM_B)[None, :]).reshape(-1)

    xt = x.reshape(t, d)
    for l in range(depth):
        xt = _ffn(xt, ffn1_norm[l].reshape(1, d), ffn1_w_gate[l].astype(BF16),
                  ffn1_w_up[l].astype(BF16), ffn1_w_down[l].astype(BF16))

        wi = w_in[l]
        kr_w = jnp.where(kr_mask[None, :], jnp.take(wi, OFF_KR + kr_cols, axis=1), 0.0)
        win_p = jnp.concatenate(
            [jnp.take(wi, OFF_QA + _SWA_COLS, axis=1), wi[:, OFF_KA:OFF_KR], kr_w], axis=1).astype(BF16)
        wuq_p = _mla_slot_cols(mla_w_uq[l], QK_DIM_B, 0).astype(BF16)
        wuk = jnp.where(np.tile(kn_mask, N_HEADS_B)[None, :], jnp.take(mla_w_ukv[l], kn_cols, axis=1), 0.0)
        wukv_p = jnp.concatenate([wuk, jnp.take(mla_w_ukv[l], v_cols, axis=1)], axis=1).astype(BF16)
        qag = (jnp.tile(swa_q_norm[l], 2) * HEAD_DIM_A ** -0.5).reshape(1, LANES)
        kag = jnp.tile(swa_k_norm[l], 2).reshape(1, LANES)
        qa, ka, va, qb, kb, vb = _mix_proj(
            xt, mix_norm[l].reshape(1, d), win_p, mla_q_lora_norm[l].reshape(1, -1), wuq_p,
            mla_kv_lora_norm[l].reshape(1, -1), wukv_p, qag, kag,
            _mla_slot_vec(mla_q_norm[l]), _mla_slot_vec(mla_k_norm[l]), cos_t, sin_t)

        oa = _swa_attn(swa_sinks[l], qa.reshape(b, s, -1), ka.reshape(b, s, -1), va.reshape(b, s, -1),
                       pos_col3, pos_blocks)
        ob = _mla_attn(qb.reshape(N_HEADS_B, b, s, LANES), kb.reshape(N_HEADS_B, b, s, LANES),
                       vb.reshape(b, s, -1))

        xt = _merge(xt, mix_norm[l].reshape(1, d), oa.reshape(t, -1), ob.reshape(t, -1),
                    wi[:, OFF_GA:].astype(BF16), jnp.take(w_branch_a[l], _SWA_COLS, axis=0).astype(BF16),
                    w_branch_b[l].astype(BF16), w_out[l].astype(BF16))

        xt = _ffn(xt, ffn2_norm[l].reshape(1, d), ffn2_w_gate[l].astype(BF16),
                  ffn2_w_up[l].astype(BF16), ffn2_w_down[l].astype(BF16))
    return xt.reshape(b, s, d)
```

```python
import numpy as np
import jax
import jax.numpy as jnp
from jax import lax
from jax.experimental import pallas as pl
from jax.experimental.pallas import tpu as pltpu

F32 = jnp.float32
BF16 = jnp.bfloat16

D_MODEL = 1024
D_FF = 2816
HEAD_DIM_A = 64
N_HEADS_A = 8
N_KV_HEADS_A = 2
GROUP_A = N_HEADS_A // N_KV_HEADS_A
WINDOW = 128
N_HEADS_B = 8
Q_LORA_RANK = 256
KV_LORA_RANK = 128
QK_NOPE_DIM = 64
QK_ROPE_DIM = 32
QK_DIM_B = QK_NOPE_DIM + QK_ROPE_DIM
V_DIM_B = 64
ROPE_BASE = 10000.0
WIDTH_A = N_HEADS_A * HEAD_DIM_A
WIDTH_B = N_HEADS_B * V_DIM_B
EPS = 1e-6
NEG = -1e30

LANES = 128
HALF = LANES // 2

OFF_QA = 0
OFF_KA = OFF_QA + WIDTH_A
OFF_VA = OFF_KA + N_KV_HEADS_A * HEAD_DIM_A
OFF_CQ = OFF_VA + N_KV_HEADS_A * HEAD_DIM_A
OFF_CKV = OFF_CQ + Q_LORA_RANK
OFF_KR = OFF_CKV + KV_LORA_RANK
OFF_GA = OFF_KR + QK_ROPE_DIM
OFF_GB = OFF_GA + D_MODEL
IN_WIDTH = OFF_GB + D_MODEL

TOKEN_TILE = 512
ATTN_BLOCK_B = 256
ROPE_TILE = 4096
VMEM_LIMIT = 56 * 1024 * 1024

ROPE_HALF = QK_ROPE_DIM // 2
SWA_IN = OFF_CQ
MLA_IN = Q_LORA_RANK + KV_LORA_RANK + QK_ROPE_DIM

_SWA_HEAD_ORDER = np.array([h for j in range(GROUP_A) for h in (j, j + GROUP_A)])
_SWA_COLS = (_SWA_HEAD_ORDER[:, None] * HEAD_DIM_A + np.arange(HEAD_DIM_A)[None, :]).reshape(-1)


def _rms(x, g, n):
    ssq = jnp.sum(x * x, axis=-1, keepdims=True)
    return x * lax.rsqrt(ssq / n + EPS) * g


def _rms_rows(c, g, n):
    ssq = jnp.sum(c * c, axis=0, keepdims=True)
    return c * lax.rsqrt(ssq / n + EPS) * g


def _dot(a, b):
    return jnp.dot(a, b, preferred_element_type=F32)


def _dot_nt(a, b):
    return lax.dot_general(a, b, (((1,), (1,)), ((), ())), preferred_element_type=F32)


def _const_spec(shape):
    return pl.BlockSpec(shape, lambda *_: (0,) * len(shape), pipeline_mode=pl.Buffered(1))


def _params(n_axes):
    return pltpu.CompilerParams(dimension_semantics=("parallel",) * n_axes,
                                vmem_limit_bytes=VMEM_LIMIT)


def _ffn_kernel(x_ref, g_ref, wg_ref, wu_ref, wd_ref, o_ref):
    x = x_ref[...]
    h = _rms(x, g_ref[...], D_MODEL).astype(BF16)
    gate = _dot(h, wg_ref[...])
    up = _dot(h, wu_ref[...])
    act = (gate * jax.nn.sigmoid(gate) * up).astype(BF16)
    o_ref[...] = x + 0.5 * _dot(act, wd_ref[...])


def _ffn(x, g, wg, wu, wd):
    t = x.shape[0]
    tm = TOKEN_TILE
    row = pl.BlockSpec((tm, D_MODEL), lambda i: (i, 0))
    return pl.pallas_call(
        _ffn_kernel,
        grid=(t // tm,),
        in_specs=[row, _const_spec((1, D_MODEL)), _const_spec((D_MODEL, D_FF)),
                  _const_spec((D_MODEL, D_FF)), _const_spec((D_FF, D_MODEL))],
        out_specs=row,
        out_shape=jax.ShapeDtypeStruct((t, D_MODEL), F32),
        compiler_params=_params(1),
        name="ffn",
    )(x, g, wg, wu, wd)


def _rope_kernel(pos_ref, invf_ref, c_ref, s_ref):
    ang = pos_ref[...].astype(F32) * invf_ref[...]
    c_ref[...] = jnp.cos(ang)
    s_ref[...] = jnp.sin(ang)


def _rope_tables(pos_row):
    t = pos_row.shape[1]
    tm = min(ROPE_TILE, t)
    inv_freq = ROPE_BASE ** (-jnp.arange(ROPE_HALF, dtype=F32) / ROPE_HALF)
    out = jax.ShapeDtypeStruct((ROPE_HALF, t), F32)
    return pl.pallas_call(
        _rope_kernel,
        grid=(t // tm,),
        in_specs=[pl.BlockSpec((1, tm), lambda i: (0, i)), _const_spec((ROPE_HALF, 1))],
        out_specs=[pl.BlockSpec((ROPE_HALF, tm), lambda i: (0, i))] * 2,
        out_shape=[out, out],
        compiler_params=_params(1),
        name="rope_tables",
    )(pos_row, inv_freq.reshape(ROPE_HALF, 1))


def _mix_proj_kernel(x_ref, g_ref, wa_ref, wm_ref, cqn_ref, wuq_ref, ckvn_ref, wukv_ref,
                     qag_ref, kag_ref, qbg_ref, kbg_ref, cos_ref, sin_ref,
                     qa_ref, ka_ref, va_ref, qb_ref, kb_ref, vb_ref):
    x = x_ref[...]
    h = _rms(x, g_ref[...], D_MODEL).astype(BF16)

    pa = _dot(h, wa_ref[...])
    lane = lax.broadcasted_iota(jnp.int32, (1, LANES), 1)
    lo = lane < HALF

    def half_norm(c, gain):
        sq = c * c
        s_lo = jnp.sum(jnp.where(lo, sq, 0.0), axis=-1, keepdims=True)
        s_hi = jnp.sum(jnp.where(lo, 0.0, sq), axis=-1, keepdims=True)
        inv = jnp.where(lo, lax.rsqrt(s_lo / HEAD_DIM_A + EPS), lax.rsqrt(s_hi / HEAD_DIM_A + EPS))
        return c * inv * gain

    for j in range(WIDTH_A // LANES):
        c = pa[:, OFF_QA + j * LANES: OFF_QA + (j + 1) * LANES]
        qa_ref[:, j * LANES:(j + 1) * LANES] = half_norm(c, qag_ref[...]).astype(BF16)
    ka_ref[...] = half_norm(pa[:, OFF_KA:OFF_VA], kag_ref[...]).astype(BF16)
    va_ref[...] = pa[:, OFF_VA:OFF_CQ].astype(BF16)

    pm = _dot_nt(wm_ref[...], h)
    cq = _rms_rows(pm[0:Q_LORA_RANK], cqn_ref[...], Q_LORA_RANK).astype(BF16)
    ckv = _rms_rows(pm[Q_LORA_RANK:Q_LORA_RANK + KV_LORA_RANK], ckvn_ref[...], KV_LORA_RANK).astype(BF16)
    kr = pm[Q_LORA_RANK + KV_LORA_RANK:MLA_IN]
    qup = _dot(wuq_ref[...], cq)
    kvup = _dot(wukv_ref[...], ckv)
    cos = cos_ref[...]
    sin = sin_ref[...]

    def norm_rope(c, gain):
        y = _rms_rows(c, gain, QK_DIM_B)
        y1 = y[QK_NOPE_DIM:QK_NOPE_DIM + ROPE_HALF]
        y2 = y[QK_NOPE_DIM + ROPE_HALF:QK_DIM_B]
        return jnp.concatenate([y[0:QK_NOPE_DIM], y1 * cos - y2 * sin, y2 * cos + y1 * sin],
                               axis=0).astype(BF16)

    per_head = QK_NOPE_DIM + V_DIM_B
    for hd in range(N_HEADS_B):
        qb_ref[hd] = norm_rope(qup[hd * QK_DIM_B:(hd + 1) * QK_DIM_B], qbg_ref[...])
        k_nope = kvup[hd * per_head:hd * per_head + QK_NOPE_DIM]
        kb_ref[hd] = norm_rope(jnp.concatenate([k_nope, kr], axis=0), kbg_ref[...])
        v = kvup[hd * per_head + QK_NOPE_DIM:(hd + 1) * per_head]
        vb_ref[hd // 2, (hd % 2) * V_DIM_B:(hd % 2 + 1) * V_DIM_B, :] = v.astype(BF16)


def _mix_proj(x, g, wa, wm, cqn, wuq, ckvn, wukv, qag, kag, qbg, kbg, cos_t, sin_t):
    t = x.shape[0]
    tm = TOKEN_TILE

    def row(w):
        return pl.BlockSpec((tm, w), lambda i: (i, 0))

    def cols(*lead):
        return pl.BlockSpec((*lead, tm), lambda i: (*(0,) * len(lead), i))

    qk_shape = jax.ShapeDtypeStruct((N_HEADS_B, QK_DIM_B, t), BF16)
    return pl.pallas_call(
        _mix_proj_kernel,
        grid=(t // tm,),
        in_specs=[row(D_MODEL), _const_spec((1, D_MODEL)),
                  _const_spec((D_MODEL, SWA_IN)), _const_spec((MLA_IN, D_MODEL)),
                  _const_spec((Q_LORA_RANK, 1)), _const_spec((N_HEADS_B * QK_DIM_B, Q_LORA_RANK)),
                  _const_spec((KV_LORA_RANK, 1)),
                  _const_spec((N_HEADS_B * (QK_NOPE_DIM + V_DIM_B), KV_LORA_RANK)),
                  _const_spec((1, LANES)), _const_spec((1, LANES)),
                  _const_spec((QK_DIM_B, 1)), _const_spec((QK_DIM_B, 1)),
                  cols(ROPE_HALF), cols(ROPE_HALF)],
        out_specs=[row(WIDTH_A), row(LANES), row(LANES),
                   cols(N_HEADS_B, QK_DIM_B), cols(N_HEADS_B, QK_DIM_B), cols(N_HEADS_B // 2, LANES)],
        out_shape=[jax.ShapeDtypeStruct((t, w), BF16) for w in (WIDTH_A, LANES, LANES)]
        + [qk_shape, qk_shape, jax.ShapeDtypeStruct((N_HEADS_B // 2, LANES, t), BF16)],
        compiler_params=_params(1),
        name="mix_proj",
    )(x, g, wa, wm, cqn, wuq, ckvn, wukv, qag, kag, qbg, kbg, cos_t, sin_t)


SWA_BLOCKS_PER_STEP = 4


def _swa_kernel(sink_ref, q_ref, k_ref, v_ref, pq_ref, pk_ref, o_ref):
    step = pl.program_id(1)
    blk = WINDOW
    lane = lax.broadcasted_iota(jnp.int32, (1, LANES), 1)
    lo = lane < HALF
    qi = lax.broadcasted_iota(jnp.int32, (blk, 2 * blk), 0) + blk
    ki = lax.broadcasted_iota(jnp.int32, (blk, 2 * blk), 1)
    diff = qi - ki
    band = (diff >= 0) & (diff < WINDOW)
    zero = jnp.zeros((), BF16)

    for r in range(SWA_BLOCKS_PER_STEP):
        cur = step * SWA_BLOCKS_PER_STEP + r
        rows = slice(r * blk, (r + 1) * blk)
        if r == 0:
            prev = jnp.maximum(cur - 1, 0)
            p0 = pl.multiple_of(prev * blk, blk)
            c0 = pl.multiple_of(cur * blk, blk)
            kk = jnp.concatenate([k_ref[0, pl.ds(p0, blk), :], k_ref[0, pl.ds(c0, blk), :]], axis=0)
            vv = jnp.concatenate([v_ref[0, pl.ds(p0, blk), :], v_ref[0, pl.ds(c0, blk), :]], axis=0)
            valid = band & ((ki >= blk) | (step > 0))
        else:
            prev = cur - 1
            p0 = pl.multiple_of(prev * blk, blk)
            kk = k_ref[0, pl.ds(p0, 2 * blk), :]
            vv = v_ref[0, pl.ds(p0, 2 * blk), :]
            valid = band
        pk = jnp.concatenate([pk_ref[0, prev], pk_ref[0, cur]], axis=1)
        dist = (pq_ref[0, rows, :] - pk).astype(F32)

        outs = [None] * GROUP_A
        for kv in range(N_KV_HEADS_A):
            keep = lo if kv == 0 else jnp.logical_not(lo)
            qs = [jnp.where(keep, q_ref[0, rows, j * LANES:(j + 1) * LANES], zero) for j in range(GROUP_A)]
            s_all = _dot_nt(jnp.concatenate(qs, axis=0), kk)
            ps, inv_ls = [], []
            for j in range(GROUP_A):
                hd = kv * GROUP_A + j
                slope = 2.0 ** (-8.0 * (hd + 1) / N_HEADS_A)
                sink = sink_ref[hd]
                s = s_all[j * blk:(j + 1) * blk] - slope * dist
                s = jnp.where(valid, s, NEG)
                m = jnp.maximum(jnp.max(s, axis=-1, keepdims=True), sink)
                p = jnp.exp(s - m)
                l = jnp.sum(p, axis=-1, keepdims=True) + jnp.exp(sink - m)
                ps.append(p.astype(BF16))
                inv_ls.append(1.0 / l)
            o_all = _dot(jnp.concatenate(ps, axis=0), vv)
            for j in range(GROUP_A):
                o = o_all[j * blk:(j + 1) * blk] * inv_ls[j]
                outs[j] = o if kv == 0 else jnp.where(lo, outs[j], o)
        for j in range(GROUP_A):
            o_ref[0, rows, j * LANES:(j + 1) * LANES] = outs[j].astype(BF16)


def _swa_attn(sinks, qa, ka, va, pos_col, pos_blocks):
    b, s, _ = qa.shape
    rows = SWA_BLOCKS_PER_STEP * WINDOW
    cur = lambda bi, i: (bi, i, 0)
    whole = lambda bi, i: (bi, 0, 0)
    return pl.pallas_call(
        _swa_kernel,
        grid=(b, s // rows),
        in_specs=[pl.BlockSpec(memory_space=pltpu.SMEM),
                  pl.BlockSpec((1, rows, WIDTH_A), cur),
                  pl.BlockSpec((1, s, LANES), whole), pl.BlockSpec((1, s, LANES), whole),
                  pl.BlockSpec((1, rows, 1), cur),
                  pl.BlockSpec((1, s // WINDOW, 1, WINDOW), lambda bi, i: (bi, 0, 0, 0))],
        out_specs=pl.BlockSpec((1, rows, WIDTH_A), cur),
        out_shape=jax.ShapeDtypeStruct((b, s, WIDTH_A), BF16),
        compiler_params=_params(2),
        name="swa_attn",
    )(sinks, qa, ka, va, pos_col, pos_blocks)


def _mla_kernel(q_ref, k_ref, v_ref, o_ref):
    s_len = q_ref.shape[2]
    tq = ATTN_BLOCK_B
    c = QK_DIM_B ** -0.5 * np.log2(np.e)
    causal = (lax.broadcasted_iota(jnp.int32, (tq, tq), 0)
              >= lax.broadcasted_iota(jnp.int32, (tq, tq), 1))
    v = v_ref[0]
    zeros = jnp.zeros((V_DIM_B, s_len), BF16)
    v_half = (jnp.concatenate([v[0:V_DIM_B], zeros], axis=0),
              jnp.concatenate([zeros, v[V_DIM_B:]], axis=0))

    for i in range(s_len // tq):
        n = (i + 1) * tq
        out = None
        for half in range(2):
            q = q_ref[half, :, i * tq:n].T
            s = _dot(q, k_ref[half, :, 0:n])
            diag = jnp.where(causal, s[:, n - tq:], NEG)
            s = diag if i == 0 else jnp.concatenate([s[:, :n - tq], diag], axis=1)
            m = jnp.max(s, axis=-1, keepdims=True)
            p = jnp.exp2((s - m) * c)
            l = jnp.sum(p, axis=-1, keepdims=True)
            o = _dot_nt(p.astype(BF16), v_half[half][:, 0:n]) / l
            out = o if half == 0 else out + o
        o_ref[0, i * tq:n, :] = out.astype(BF16)


def _mla_attn(qb, kb, vb, b, s):
    qk_spec = pl.BlockSpec((2, QK_DIM_B, s), lambda bi, j: (j, 0, bi))
    return pl.pallas_call(
        _mla_kernel,
        grid=(b, N_HEADS_B // 2),
        in_specs=[qk_spec, qk_spec, pl.BlockSpec((1, LANES, s), lambda bi, j: (j, 0, bi))],
        out_specs=pl.BlockSpec((1, s, LANES), lambda bi, j: (bi, 0, j)),
        out_shape=jax.ShapeDtypeStruct((b, s, WIDTH_B), BF16),
        compiler_params=_params(2),
        name="mla_attn",
    )(qb, kb, vb)


def _merge_kernel(x_ref, g_ref, oa_ref, ob_ref, wg_ref, wba_ref, wbb_ref, wo_ref, o_ref):
    x = x_ref[...]
    h = _rms(x, g_ref[...], D_MODEL).astype(BF16)
    gates = _dot(h, wg_ref[...])
    ma = _dot(oa_ref[...], wba_ref[...])
    mb = _dot(ob_ref[...], wbb_ref[...])
    merged = jax.nn.sigmoid(gates[:, :D_MODEL]) * ma + jax.nn.sigmoid(gates[:, D_MODEL:]) * mb
    o_ref[...] = x + _dot(merged.astype(BF16), wo_ref[...])


def _merge(x, g, oa, ob, wg, wba, wbb, wo):
    t = x.shape[0]
    tm = TOKEN_TILE

    def row(w):
        return pl.BlockSpec((tm, w), lambda i: (i, 0))

    return pl.pallas_call(
        _merge_kernel,
        grid=(t // tm,),
        in_specs=[row(D_MODEL), _const_spec((1, D_MODEL)), row(WIDTH_A), row(WIDTH_B),
                  _const_spec((D_MODEL, 2 * D_MODEL)), _const_spec((WIDTH_A, D_MODEL)),
                  _const_spec((WIDTH_B, D_MODEL)), _const_spec((D_MODEL, D_MODEL))],
        out_specs=row(D_MODEL),
        out_shape=jax.ShapeDtypeStruct((t, D_MODEL), F32),
        compiler_params=_params(1),
        name="merge",
    )(x, g, oa, ob, wg, wba, wbb, wo)


def kernel(x, positions, ffn1_norm, ffn1_w_gate, ffn1_w_up, ffn1_w_down, mix_norm, w_in, swa_q_norm, swa_k_norm, swa_sinks, mla_q_lora_norm, mla_w_uq, mla_kv_lora_norm, mla_w_ukv, mla_q_norm, mla_k_norm, w_branch_a, w_branch_b, w_out, ffn2_norm, ffn2_w_gate, ffn2_w_up, ffn2_w_down):
    b, s, d = x.shape
    t = b * s
    depth = w_in.shape[0]
    assert d == D_MODEL and s % ATTN_BLOCK_B == 0 and t % TOKEN_TILE == 0 and t % min(ROPE_TILE, t) == 0
    assert s % (SWA_BLOCKS_PER_STEP * WINDOW) == 0

    cos_t, sin_t = _rope_tables(positions.reshape(1, t))
    pos_col3 = positions.reshape(b, s, 1)
    pos_blocks = positions.reshape(b, s // WINDOW, 1, WINDOW)

    xt = x.reshape(t, d)
    for l in range(depth):
        xt = _ffn(xt, ffn1_norm[l].reshape(1, d), ffn1_w_gate[l].astype(BF16),
                  ffn1_w_up[l].astype(BF16), ffn1_w_down[l].astype(BF16))

        wi = w_in[l]
        wa = jnp.concatenate([jnp.take(wi, OFF_QA + _SWA_COLS, axis=1), wi[:, OFF_KA:OFF_CQ]],
                             axis=1).astype(BF16)
        wm = wi[:, OFF_CQ:OFF_GA].T.astype(BF16)
        qag = (jnp.tile(swa_q_norm[l], 2) * HEAD_DIM_A ** -0.5).reshape(1, LANES)
        kag = jnp.tile(swa_k_norm[l], 2).reshape(1, LANES)
        qa, ka, va, qb, kb, vb = _mix_proj(
            xt, mix_norm[l].reshape(1, d), wa, wm,
            mla_q_lora_norm[l].reshape(-1, 1), mla_w_uq[l].T.astype(BF16),
            mla_kv_lora_norm[l].reshape(-1, 1), mla_w_ukv[l].T.astype(BF16), qag, kag,
            mla_q_norm[l].reshape(-1, 1), mla_k_norm[l].reshape(-1, 1), cos_t, sin_t)

        oa = _swa_attn(swa_sinks[l], qa.reshape(b, s, -1), ka.reshape(b, s, -1), va.reshape(b, s, -1),
                       pos_col3, pos_blocks)
        ob = _mla_attn(qb, kb, vb, b, s)

        xt = _merge(xt, mix_norm[l].reshape(1, d), oa.reshape(t, -1), ob.reshape(t, -1),
                    wi[:, OFF_GA:].astype(BF16), jnp.take(w_branch_a[l], _SWA_COLS, axis=0).astype(BF16),
                    w_branch_b[l].astype(BF16), w_out[l].astype(BF16))

        xt = _ffn(xt, ffn2_norm[l].reshape(1, d), ffn2_w_gate[l].astype(BF16),
                  ffn2_w_up[l].astype(BF16), ffn2_w_down[l].astype(BF16))
    return xt.reshape(b, s, d)
```

```python
import numpy as np
import jax
import jax.numpy as jnp
from jax import lax
from jax.experimental import pallas as pl
from jax.experimental.pallas import tpu as pltpu

F32 = jnp.float32
BF16 = jnp.bfloat16

D_MODEL = 1024
D_FF = 2816
HEAD_DIM_A = 64
N_HEADS_A = 8
N_KV_HEADS_A = 2
GROUP_A = N_HEADS_A // N_KV_HEADS_A
WINDOW = 128
N_HEADS_B = 8
Q_LORA_RANK = 256
KV_LORA_RANK = 128
QK_NOPE_DIM = 64
QK_ROPE_DIM = 32
QK_DIM_B = QK_NOPE_DIM + QK_ROPE_DIM
V_DIM_B = 64
ROPE_BASE = 10000.0
WIDTH_A = N_HEADS_A * HEAD_DIM_A
WIDTH_B = N_HEADS_B * V_DIM_B
EPS = 1e-6
NEG = -1e30
MASK_DIST = 1e30

LANES = 128
HALF = LANES // 2

OFF_QA = 0
OFF_KA = OFF_QA + WIDTH_A
OFF_VA = OFF_KA + N_KV_HEADS_A * HEAD_DIM_A
OFF_CQ = OFF_VA + N_KV_HEADS_A * HEAD_DIM_A
OFF_CKV = OFF_CQ + Q_LORA_RANK
OFF_KR = OFF_CKV + KV_LORA_RANK
OFF_GA = OFF_KR + QK_ROPE_DIM
OFF_GB = OFF_GA + D_MODEL
IN_WIDTH = OFF_GB + D_MODEL

TOKEN_TILE = 512
ATTN_BLOCK_B = 256
MLA_PAIRS_PER_STEP = 1
ROPE_TILE = 4096
VMEM_LIMIT = 56 * 1024 * 1024

ROPE_HALF = QK_ROPE_DIM // 2
SWA_IN = OFF_CQ
MLA_IN = Q_LORA_RANK + KV_LORA_RANK + QK_ROPE_DIM

_SWA_HEAD_ORDER = np.array([h for j in range(GROUP_A) for h in (j, j + GROUP_A)])
_SWA_COLS = (_SWA_HEAD_ORDER[:, None] * HEAD_DIM_A + np.arange(HEAD_DIM_A)[None, :]).reshape(-1)


def _rms(x, g, n):
    ssq = jnp.sum(x * x, axis=-1, keepdims=True)
    return x * lax.rsqrt(ssq / n + EPS) * g


def _rms_rows(c, g, n):
    ssq = jnp.sum(c * c, axis=0, keepdims=True)
    return c * lax.rsqrt(ssq / n + EPS) * g


def _dot(a, b):
    return jnp.dot(a, b, preferred_element_type=F32)


def _dot_nt(a, b):
    return lax.dot_general(a, b, (((1,), (1,)), ((), ())), preferred_element_type=F32)


def _const_spec(shape):
    return pl.BlockSpec(shape, lambda *_: (0,) * len(shape), pipeline_mode=pl.Buffered(1))


def _params(n_axes):
    return pltpu.CompilerParams(dimension_semantics=("parallel",) * n_axes,
                                vmem_limit_bytes=VMEM_LIMIT)


def _ffn_kernel(x_ref, g_ref, wg_ref, wu_ref, wd_ref, o_ref):
    x = x_ref[...]
    h = _rms(x, g_ref[...], D_MODEL).astype(BF16)
    gate = _dot(h, wg_ref[...])
    up = _dot(h, wu_ref[...])
    act = (gate * jax.nn.sigmoid(gate) * up).astype(BF16)
    o_ref[...] = x + 0.5 * _dot(act, wd_ref[...])


def _ffn(x, g, wg, wu, wd):
    t = x.shape[0]
    tm = TOKEN_TILE
    row = pl.BlockSpec((tm, D_MODEL), lambda i: (i, 0))
    return pl.pallas_call(
        _ffn_kernel,
        grid=(t // tm,),
        in_specs=[row, _const_spec((1, D_MODEL)), _const_spec((D_MODEL, D_FF)),
                  _const_spec((D_MODEL, D_FF)), _const_spec((D_FF, D_MODEL))],
        out_specs=row,
        out_shape=jax.ShapeDtypeStruct((t, D_MODEL), F32),
        compiler_params=_params(1),
        name="ffn",
    )(x, g, wg, wu, wd)


def _rope_kernel(pos_ref, invf_ref, c_ref, s_ref):
    ang = pos_ref[...].astype(F32) * invf_ref[...]
    c_ref[...] = jnp.cos(ang)
    s_ref[...] = jnp.sin(ang)


def _rope_tables(pos_row):
    t = pos_row.shape[1]
    tm = min(ROPE_TILE, t)
    inv_freq = ROPE_BASE ** (-jnp.arange(ROPE_HALF, dtype=F32) / ROPE_HALF)
    out = jax.ShapeDtypeStruct((ROPE_HALF, t), F32)
    return pl.pallas_call(
        _rope_kernel,
        grid=(t // tm,),
        in_specs=[pl.BlockSpec((1, tm), lambda i: (0, i)), _const_spec((ROPE_HALF, 1))],
        out_specs=[pl.BlockSpec((ROPE_HALF, tm), lambda i: (0, i))] * 2,
        out_shape=[out, out],
        compiler_params=_params(1),
        name="rope_tables",
    )(pos_row, inv_freq.reshape(ROPE_HALF, 1))


def _mix_proj_kernel(x_ref, g_ref, wa_ref, wm_ref, cqn_ref, wuq_ref, ckvn_ref, wukv_ref,
                     qag_ref, kag_ref, qbg_ref, kbg_ref, cos_ref, sin_ref,
                     qa_ref, ka_ref, va_ref, qb_ref, kb_ref, vb_ref):
    x = x_ref[...]
    h = _rms(x, g_ref[...], D_MODEL).astype(BF16)

    pa = _dot(h, wa_ref[...])
    lane = lax.broadcasted_iota(jnp.int32, (1, LANES), 1)
    lo = lane < HALF

    def half_norm(c, gain):
        sq = c * c
        s_lo = jnp.sum(jnp.where(lo, sq, 0.0), axis=-1, keepdims=True)
        s_hi = jnp.sum(jnp.where(lo, 0.0, sq), axis=-1, keepdims=True)
        inv = jnp.where(lo, lax.rsqrt(s_lo / HEAD_DIM_A + EPS), lax.rsqrt(s_hi / HEAD_DIM_A + EPS))
        return c * inv * gain

    for j in range(WIDTH_A // LANES):
        c = pa[:, OFF_QA + j * LANES: OFF_QA + (j + 1) * LANES]
        qa_ref[:, j * LANES:(j + 1) * LANES] = half_norm(c, qag_ref[...]).astype(BF16)
    ka_ref[...] = half_norm(pa[:, OFF_KA:OFF_VA], kag_ref[...]).astype(BF16)
    va_ref[...] = pa[:, OFF_VA:OFF_CQ].astype(BF16)

    pm = _dot_nt(wm_ref[...], h)
    cq = _rms_rows(pm[0:Q_LORA_RANK], cqn_ref[...], Q_LORA_RANK).astype(BF16)
    ckv = _rms_rows(pm[Q_LORA_RANK:Q_LORA_RANK + KV_LORA_RANK], ckvn_ref[...], KV_LORA_RANK).astype(BF16)
    kr = pm[Q_LORA_RANK + KV_LORA_RANK:MLA_IN]
    qup = _dot(wuq_ref[...], cq)
    kvup = _dot(wukv_ref[...], ckv)
    cos = cos_ref[...]
    sin = sin_ref[...]

    def norm_rope(c, gain):
        y = _rms_rows(c, gain, QK_DIM_B)
        y1 = y[QK_NOPE_DIM:QK_NOPE_DIM + ROPE_HALF]
        y2 = y[QK_NOPE_DIM + ROPE_HALF:QK_DIM_B]
        return jnp.concatenate([y[0:QK_NOPE_DIM], y1 * cos - y2 * sin, y2 * cos + y1 * sin],
                               axis=0).astype(BF16)

    per_head = QK_NOPE_DIM + V_DIM_B
    for hd in range(N_HEADS_B):
        qb_ref[hd] = norm_rope(qup[hd * QK_DIM_B:(hd + 1) * QK_DIM_B], qbg_ref[...])
        k_nope = kvup[hd * per_head:hd * per_head + QK_NOPE_DIM]
        kb_ref[hd] = norm_rope(jnp.concatenate([k_nope, kr], axis=0), kbg_ref[...])
        v = kvup[hd * per_head + QK_NOPE_DIM:(hd + 1) * per_head]
        vb_ref[hd // 2, (hd % 2) * V_DIM_B:(hd % 2 + 1) * V_DIM_B, :] = v.astype(BF16)


def _mix_proj(x, g, wa, wm, cqn, wuq, ckvn, wukv, qag, kag, qbg, kbg, cos_t, sin_t):
    t = x.shape[0]
    tm = TOKEN_TILE

    def row(w):
        return pl.BlockSpec((tm, w), lambda i: (i, 0))

    def cols(*lead):
        return pl.BlockSpec((*lead, tm), lambda i: (*(0,) * len(lead), i))

    qk_shape = jax.ShapeDtypeStruct((N_HEADS_B, QK_DIM_B, t), BF16)
    return pl.pallas_call(
        _mix_proj_kernel,
        grid=(t // tm,),
        in_specs=[row(D_MODEL), _const_spec((1, D_MODEL)),
                  _const_spec((D_MODEL, SWA_IN)), _const_spec((MLA_IN, D_MODEL)),
                  _const_spec((Q_LORA_RANK, 1)), _const_spec((N_HEADS_B * QK_DIM_B, Q_LORA_RANK)),
                  _const_spec((KV_LORA_RANK, 1)),
                  _const_spec((N_HEADS_B * (QK_NOPE_DIM + V_DIM_B), KV_LORA_RANK)),
                  _const_spec((1, LANES)), _const_spec((1, LANES)),
                  _const_spec((QK_DIM_B, 1)), _const_spec((QK_DIM_B, 1)),
                  cols(ROPE_HALF), cols(ROPE_HALF)],
        out_specs=[row(WIDTH_A), row(LANES), row(LANES),
                   cols(N_HEADS_B, QK_DIM_B), cols(N_HEADS_B, QK_DIM_B), cols(N_HEADS_B // 2, LANES)],
        out_shape=[jax.ShapeDtypeStruct((t, w), BF16) for w in (WIDTH_A, LANES, LANES)]
        + [qk_shape, qk_shape, jax.ShapeDtypeStruct((N_HEADS_B // 2, LANES, t), BF16)],
        compiler_params=_params(1),
        name="mix_proj",
    )(x, g, wa, wm, cqn, wuq, ckvn, wukv, qag, kag, qbg, kbg, cos_t, sin_t)


SWA_BLOCKS_PER_STEP = 4


def _swa_kernel(sink_ref, q_ref, k_ref, v_ref, pq_ref, pk_ref, o_ref):
    step = pl.program_id(1)
    blk = WINDOW
    lane = lax.broadcasted_iota(jnp.int32, (1, LANES), 1)
    lo = lane < HALF
    qi = lax.broadcasted_iota(jnp.int32, (blk, 2 * blk), 0) + blk
    ki = lax.broadcasted_iota(jnp.int32, (blk, 2 * blk), 1)
    diff = qi - ki
    band = (diff >= 0) & (diff < WINDOW)
    zero = jnp.zeros((), BF16)

    for r in range(SWA_BLOCKS_PER_STEP):
        cur = step * SWA_BLOCKS_PER_STEP + r
        rows = slice(r * blk, (r + 1) * blk)
        if r == 0:
            prev = jnp.maximum(cur - 1, 0)
            p0 = pl.multiple_of(prev * blk, blk)
            c0 = pl.multiple_of(cur * blk, blk)
            kk = jnp.concatenate([k_ref[0, pl.ds(p0, blk), :], k_ref[0, pl.ds(c0, blk), :]], axis=0)
            vv = jnp.concatenate([v_ref[0, pl.ds(p0, blk), :], v_ref[0, pl.ds(c0, blk), :]], axis=0)
            valid = band & ((ki >= blk) | (step > 0))
        else:
            prev = cur - 1
            p0 = pl.multiple_of(prev * blk, blk)
            kk = k_ref[0, pl.ds(p0, 2 * blk), :]
            vv = v_ref[0, pl.ds(p0, 2 * blk), :]
            valid = band
        pk = jnp.concatenate([pk_ref[0, prev], pk_ref[0, cur]], axis=1)
        dist = jnp.where(valid, (pq_ref[0, rows, :] - pk).astype(F32), MASK_DIST)

        o_kv, sink_kv = [], []
        for kv in range(N_KV_HEADS_A):
            keep = lo if kv == 0 else jnp.logical_not(lo)
            qs = [jnp.where(keep, q_ref[0, rows, j * LANES:(j + 1) * LANES], zero) for j in range(GROUP_A)]
            s_all = _dot_nt(jnp.concatenate(qs, axis=0), kk)
            ps, sink_terms = [], []
            for j in range(GROUP_A):
                hd = kv * GROUP_A + j
                slope = 2.0 ** (-8.0 * (hd + 1) / N_HEADS_A)
                sink = sink_ref[hd]
                s = s_all[j * blk:(j + 1) * blk] - slope * dist
                m = jnp.maximum(jnp.max(s, axis=-1, keepdims=True), sink)
                ps.append(jnp.exp(s - m).astype(BF16))
                sink_terms.append(jnp.exp(sink - m))
            v_ext = jnp.where(keep, vv, jnp.ones((), BF16))
            o_kv.append(_dot(jnp.concatenate(ps, axis=0), v_ext))
            sink_kv.append(sink_terms)
        for j in range(GROUP_A):
            o0 = o_kv[0][j * blk:(j + 1) * blk]
            o1 = o_kv[1][j * blk:(j + 1) * blk]
            num = jnp.where(lo, o0, o1)
            den = pltpu.roll(jnp.where(lo, o1, o0), HALF, 1) + jnp.where(lo, sink_kv[0][j], sink_kv[1][j])
            o_ref[0, rows, j * LANES:(j + 1) * LANES] = (num / den).astype(BF16)


def _swa_attn(sinks, qa, ka, va, pos_col, pos_blocks):
    b, s, _ = qa.shape
    rows = SWA_BLOCKS_PER_STEP * WINDOW
    cur = lambda bi, i: (bi, i, 0)
    whole = lambda bi, i: (bi, 0, 0)
    return pl.pallas_call(
        _swa_kernel,
        grid=(b, s // rows),
        in_specs=[pl.BlockSpec(memory_space=pltpu.SMEM),
                  pl.BlockSpec((1, rows, WIDTH_A), cur),
                  pl.BlockSpec((1, s, LANES), whole), pl.BlockSpec((1, s, LANES), whole),
                  pl.BlockSpec((1, rows, 1), cur),
                  pl.BlockSpec((1, s // WINDOW, 1, WINDOW), lambda bi, i: (bi, 0, 0, 0))],
        out_specs=pl.BlockSpec((1, rows, WIDTH_A), cur),
        out_shape=jax.ShapeDtypeStruct((b, s, WIDTH_A), BF16),
        compiler_params=_params(2),
        name="swa_attn",
    )(sinks, qa, ka, va, pos_col, pos_blocks)


def _mla_kernel(q_ref, k_ref, v_ref, o_ref):
    s_len = q_ref.shape[2]
    tq = ATTN_BLOCK_B
    c = QK_DIM_B ** -0.5 * np.log2(np.e)
    causal = (lax.broadcasted_iota(jnp.int32, (tq, tq), 0)
              >= lax.broadcasted_iota(jnp.int32, (tq, tq), 1))
    ones = jnp.ones((V_DIM_B, s_len), BF16)
    lo = lax.broadcasted_iota(jnp.int32, (1, LANES), 1) < HALF

    for pair in range(MLA_PAIRS_PER_STEP):
        v = v_ref[pair]
        v_ext = (jnp.concatenate([v[0:V_DIM_B], ones], axis=0),
                 jnp.concatenate([ones, v[V_DIM_B:]], axis=0))
        for i in reversed(range(s_len // tq)):
            n = (i + 1) * tq
            o = []
            for half in range(2):
                hd = 2 * pair + half
                q = q_ref[hd, :, i * tq:n].T
                s = _dot(q, k_ref[hd, :, 0:n])
                diag = jnp.where(causal, s[:, n - tq:], NEG)
                s = diag if i == 0 else jnp.concatenate([s[:, :n - tq], diag], axis=1)
                m = jnp.max(s, axis=-1, keepdims=True)
                p = jnp.exp2((s - m) * c).astype(BF16)
                o.append(_dot_nt(p, v_ext[half][:, 0:n]))
            num = jnp.where(lo, o[0], o[1])
            den = pltpu.roll(jnp.where(lo, o[1], o[0]), HALF, 1)
            o_ref[0, i * tq:n, pair * LANES:(pair + 1) * LANES] = (num / den).astype(BF16)


def _mla_attn(qb, kb, vb, b, s):
    pp = MLA_PAIRS_PER_STEP
    qk_spec = pl.BlockSpec((2 * pp, QK_DIM_B, s), lambda bi, j: (j, 0, bi))
    return pl.pallas_call(
        _mla_kernel,
        grid=(b, N_HEADS_B // (2 * pp)),
        in_specs=[qk_spec, qk_spec, pl.BlockSpec((pp, LANES, s), lambda bi, j: (j, 0, bi))],
        out_specs=pl.BlockSpec((1, s, pp * LANES), lambda bi, j: (bi, 0, j)),
        out_shape=jax.ShapeDtypeStruct((b, s, WIDTH_B), BF16),
        compiler_params=_params(2),
        name="mla_attn",
    )(qb, kb, vb)


def _merge_kernel(x_ref, g_ref, oa_ref, ob_ref, wg_ref, wba_ref, wbb_ref, wo_ref, o_ref):
    x = x_ref[...]
    h = _rms(x, g_ref[...], D_MODEL).astype(BF16)
    gates = _dot(h, wg_ref[...])
    ma = _dot(oa_ref[...], wba_ref[...])
    mb = _dot(ob_ref[...], wbb_ref[...])
    merged = jax.nn.sigmoid(gates[:, :D_MODEL]) * ma + jax.nn.sigmoid(gates[:, D_MODEL:]) * mb
    o_ref[...] = x + _dot(merged.astype(BF16), wo_ref[...])


def _merge(x, g, oa, ob, wg, wba, wbb, wo):
    t = x.shape[0]
    tm = TOKEN_TILE

    def row(w):
        return pl.BlockSpec((tm, w), lambda i: (i, 0))

    return pl.pallas_call(
        _merge_kernel,
        grid=(t // tm,),
        in_specs=[row(D_MODEL), _const_spec((1, D_MODEL)), row(WIDTH_A), row(WIDTH_B),
                  _const_spec((D_MODEL, 2 * D_MODEL)), _const_spec((WIDTH_A, D_MODEL)),
                  _const_spec((WIDTH_B, D_MODEL)), _const_spec((D_MODEL, D_MODEL))],
        out_specs=row(D_MODEL),
        out_shape=jax.ShapeDtypeStruct((t, D_MODEL), F32),
        compiler_params=_params(1),
        name="merge",
    )(x, g, oa, ob, wg, wba, wbb, wo)


def kernel(x, positions, ffn1_norm, ffn1_w_gate, ffn1_w_up, ffn1_w_down, mix_norm, w_in, swa_q_norm, swa_k_norm, swa_sinks, mla_q_lora_norm, mla_w_uq, mla_kv_lora_norm, mla_w_ukv, mla_q_norm, mla_k_norm, w_branch_a, w_branch_b, w_out, ffn2_norm, ffn2_w_gate, ffn2_w_up, ffn2_w_down):
    b, s, d = x.shape
    t = b * s
    depth = w_in.shape[0]
    assert d == D_MODEL and s % ATTN_BLOCK_B == 0 and t % TOKEN_TILE == 0 and t % min(ROPE_TILE, t) == 0
    assert s % (SWA_BLOCKS_PER_STEP * WINDOW) == 0

    cos_t, sin_t = _rope_tables(positions.reshape(1, t))
    pos_col3 = positions.reshape(b, s, 1)
    pos_blocks = positions.reshape(b, s // WINDOW, 1, WINDOW)

    xt = x.reshape(t, d)
    for l in range(depth):
        xt = _ffn(xt, ffn1_norm[l].reshape(1, d), ffn1_w_gate[l].astype(BF16),
                  ffn1_w_up[l].astype(BF16), ffn1_w_down[l].astype(BF16))

        wi = w_in[l]
        wa = jnp.concatenate([jnp.take(wi, OFF_QA + _SWA_COLS, axis=1), wi[:, OFF_KA:OFF_CQ]],
                             axis=1).astype(BF16)
        wm = wi[:, OFF_CQ:OFF_GA].T.astype(BF16)
        qag = (jnp.tile(swa_q_norm[l], 2) * HEAD_DIM_A ** -0.5).reshape(1, LANES)
        kag = jnp.tile(swa_k_norm[l], 2).reshape(1, LANES)
        qa, ka, va, qb, kb, vb = _mix_proj(
            xt, mix_norm[l].reshape(1, d), wa, wm,
            mla_q_lora_norm[l].reshape(-1, 1), mla_w_uq[l].T.astype(BF16),
            mla_kv_lora_norm[l].reshape(-1, 1), mla_w_ukv[l].T.astype(BF16), qag, kag,
            mla_q_norm[l].reshape(-1, 1), mla_k_norm[l].reshape(-1, 1), cos_t, sin_t)

        oa = _swa_attn(swa_sinks[l], qa.reshape(b, s, -1), ka.reshape(b, s, -1), va.reshape(b, s, -1),
                       pos_col3, pos_blocks)
        ob = _mla_attn(qb, kb, vb, b, s)

        xt = _merge(xt, mix_norm[l].reshape(1, d), oa.reshape(t, -1), ob.reshape(t, -1),
                    wi[:, OFF_GA:].astype(BF16), jnp.take(w_branch_a[l], _SWA_COLS, axis=0).astype(BF16),
                    w_branch_b[l].astype(BF16), w_out[l].astype(BF16))

        xt = _ffn(xt, ffn2_norm[l].reshape(1, d), ffn2_w_gate[l].astype(BF16),
                  ffn2_w_up[l].astype(BF16), ffn2_w_down[l].astype(BF16))
    return xt.reshape(b, s, d)
```

```python
import numpy as np
import jax
import jax.numpy as jnp
from jax import lax
from jax.experimental import pallas as pl
from jax.experimental.pallas import tpu as pltpu

F32 = jnp.float32
BF16 = jnp.bfloat16

D_MODEL = 1024
D_FF = 2816
HEAD_DIM_A = 64
N_HEADS_A = 8
N_KV_HEADS_A = 2
GROUP_A = N_HEADS_A // N_KV_HEADS_A
WINDOW = 128
N_HEADS_B = 8
Q_LORA_RANK = 256
KV_LORA_RANK = 128
QK_NOPE_DIM = 64
QK_ROPE_DIM = 32
QK_DIM_B = QK_NOPE_DIM + QK_ROPE_DIM
V_DIM_B = 64
ROPE_BASE = 10000.0
WIDTH_A = N_HEADS_A * HEAD_DIM_A
WIDTH_B = N_HEADS_B * V_DIM_B
EPS = 1e-6
NEG = -1e30
MASK_DIST = 1e30

LANES = 128
HALF = LANES // 2
BF16_SUBLANES = 16

OFF_QA = 0
OFF_KA = OFF_QA + WIDTH_A
OFF_VA = OFF_KA + N_KV_HEADS_A * HEAD_DIM_A
OFF_CQ = OFF_VA + N_KV_HEADS_A * HEAD_DIM_A
OFF_CKV = OFF_CQ + Q_LORA_RANK
OFF_KR = OFF_CKV + KV_LORA_RANK
OFF_GA = OFF_KR + QK_ROPE_DIM
OFF_GB = OFF_GA + D_MODEL
IN_WIDTH = OFF_GB + D_MODEL

TOKEN_TILE = 1024
FF_CHUNK = 256
ATTN_BLOCK_B = 256
MLA_PAIRS_PER_STEP = 2
ROPE_TILE = 4096
VMEM_LIMIT = 56 * 1024 * 1024

ROPE_HALF = QK_ROPE_DIM // 2
SWA_IN = OFF_CQ
MLA_IN = Q_LORA_RANK + KV_LORA_RANK + QK_ROPE_DIM

_SWA_HEAD_ORDER = np.array([h for j in range(GROUP_A) for h in (j, j + GROUP_A)])
_SWA_COLS = (_SWA_HEAD_ORDER[:, None] * HEAD_DIM_A + np.arange(HEAD_DIM_A)[None, :]).reshape(-1)


def _rms(x, g, n):
    ssq = jnp.sum(x * x, axis=-1, keepdims=True)
    return x * lax.rsqrt(ssq / n + EPS) * g


def _rms_rows(c, g, n):
    ssq = jnp.sum(c * c, axis=0, keepdims=True)
    return c * lax.rsqrt(ssq / n + EPS) * g


def _dot(a, b):
    return jnp.dot(a, b, preferred_element_type=F32)


def _dot_nt(a, b):
    return lax.dot_general(a, b, (((1,), (1,)), ((), ())), preferred_element_type=F32)


def _const_spec(shape):
    return pl.BlockSpec(shape, lambda *_: (0,) * len(shape), pipeline_mode=pl.Buffered(1))


def _params(n_axes):
    return pltpu.CompilerParams(dimension_semantics=("parallel",) * n_axes,
                                vmem_limit_bytes=VMEM_LIMIT)


def _ffn_kernel(x_ref, g_ref, wg_ref, wu_ref, wd_ref, *rest):
    o_ref = rest[len(rest) // 2]
    x = x_ref[...]
    h = _rms(x, g_ref[...], D_MODEL).astype(BF16)
    y = None
    for c in range(D_FF // FF_CHUNK):
        sl = slice(c * FF_CHUNK, (c + 1) * FF_CHUNK)
        gate = _dot(h, wg_ref[:, sl])
        up = _dot(h, wu_ref[:, sl])
        act = (gate * jax.nn.sigmoid(gate) * up).astype(BF16)
        d = _dot(act, wd_ref[sl, :])
        y = d if y is None else y + d
    o_ref[...] = x + 0.5 * y
    for src_ref, dst_ref in zip(rest[:len(rest) // 2], rest[len(rest) // 2 + 1:]):
        dst_ref[...] = src_ref[...].astype(BF16)


def _slab_spec(rows, cols, n_steps):
    blk = next(r for r in range(BF16_SUBLANES, rows + 1, BF16_SUBLANES)
               if rows % r == 0 and rows // r <= n_steps)
    last = rows // blk - 1
    return pl.BlockSpec((blk, cols), lambda i: (jnp.minimum(i, last), 0))


def _ffn(x, g, wg, wu, wd, next_weights=()):
    t = x.shape[0]
    tm = TOKEN_TILE
    n_steps = t // tm
    row = pl.BlockSpec((tm, D_MODEL), lambda i: (i, 0))
    slabs = [_slab_spec(*w.shape, n_steps) for w in next_weights]
    out = pl.pallas_call(
        _ffn_kernel,
        grid=(n_steps,),
        in_specs=[row, _const_spec((1, D_MODEL)), _const_spec((D_MODEL, D_FF)),
                  _const_spec((D_MODEL, D_FF)), _const_spec((D_FF, D_MODEL))] + slabs,
        out_specs=[row] + slabs,
        out_shape=[jax.ShapeDtypeStruct((t, D_MODEL), F32)]
        + [jax.ShapeDtypeStruct(w.shape, BF16) for w in next_weights],
        compiler_params=pltpu.CompilerParams(dimension_semantics=("arbitrary",),
                                             vmem_limit_bytes=VMEM_LIMIT),
        name="ffn",
    )(x, g, wg, wu, wd, *next_weights)
    return out[0], tuple(out[1:])


def _rope_kernel(pos_ref, invf_ref, c_ref, s_ref):
    ang = pos_ref[...].astype(F32) * invf_ref[...]
    c_ref[...] = jnp.cos(ang)
    s_ref[...] = jnp.sin(ang)


def _rope_tables(pos_row):
    t = pos_row.shape[1]
    tm = min(ROPE_TILE, t)
    inv_freq = ROPE_BASE ** (-jnp.arange(ROPE_HALF, dtype=F32) / ROPE_HALF)
    out = jax.ShapeDtypeStruct((ROPE_HALF, t), F32)
    return pl.pallas_call(
        _rope_kernel,
        grid=(t // tm,),
        in_specs=[pl.BlockSpec((1, tm), lambda i: (0, i)), _const_spec((ROPE_HALF, 1))],
        out_specs=[pl.BlockSpec((ROPE_HALF, tm), lambda i: (0, i))] * 2,
        out_shape=[out, out],
        compiler_params=_params(1),
        name="rope_tables",
    )(pos_row, inv_freq.reshape(ROPE_HALF, 1))


def _mix_proj_kernel(x_ref, g_ref, wa_ref, wm_ref, cqn_ref, wuq_ref, ckvn_ref, wukv_ref,
                     qag_ref, kag_ref, qbg_ref, kbg_ref, cos_ref, sin_ref,
                     qa_ref, ka_ref, va_ref, qb_ref, kb_ref, vb_ref):
    x = x_ref[...]
    h = _rms(x, g_ref[...], D_MODEL).astype(BF16)

    pa = _dot(h, wa_ref[...])
    lane = lax.broadcasted_iota(jnp.int32, (1, LANES), 1)
    lo = lane < HALF

    def half_norm(c, gain):
        sq = c * c
        s_lo = jnp.sum(jnp.where(lo, sq, 0.0), axis=-1, keepdims=True)
        s_hi = jnp.sum(jnp.where(lo, 0.0, sq), axis=-1, keepdims=True)
        inv = jnp.where(lo, lax.rsqrt(s_lo / HEAD_DIM_A + EPS), lax.rsqrt(s_hi / HEAD_DIM_A + EPS))
        return c * inv * gain

    for j in range(WIDTH_A // LANES):
        c = pa[:, OFF_QA + j * LANES: OFF_QA + (j + 1) * LANES]
        qa_ref[:, j * LANES:(j + 1) * LANES] = half_norm(c, qag_ref[...]).astype(BF16)
    ka_ref[...] = half_norm(pa[:, OFF_KA:OFF_VA], kag_ref[...]).astype(BF16)
    va_ref[...] = pa[:, OFF_VA:OFF_CQ].astype(BF16)

    pm = _dot_nt(wm_ref[...], h)
    cq = _rms_rows(pm[0:Q_LORA_RANK], cqn_ref[...], Q_LORA_RANK).astype(BF16)
    ckv = _rms_rows(pm[Q_LORA_RANK:Q_LORA_RANK + KV_LORA_RANK], ckvn_ref[...], KV_LORA_RANK).astype(BF16)
    kr = pm[Q_LORA_RANK + KV_LORA_RANK:MLA_IN]
    qup = _dot(wuq_ref[...], cq)
    kvup = _dot(wukv_ref[...], ckv)
    cos = cos_ref[...]
    sin = sin_ref[...]

    def norm_rope(c, gain):
        y = _rms_rows(c, gain, QK_DIM_B)
        y1 = y[QK_NOPE_DIM:QK_NOPE_DIM + ROPE_HALF]
        y2 = y[QK_NOPE_DIM + ROPE_HALF:QK_DIM_B]
        return jnp.concatenate([y[0:QK_NOPE_DIM], y1 * cos - y2 * sin, y2 * cos + y1 * sin],
                               axis=0).astype(BF16)

    per_head = QK_NOPE_DIM + V_DIM_B
    for hd in range(N_HEADS_B):
        qb_ref[hd] = norm_rope(qup[hd * QK_DIM_B:(hd + 1) * QK_DIM_B], qbg_ref[...])
        k_nope = kvup[hd * per_head:hd * per_head + QK_NOPE_DIM]
        kb_ref[hd] = norm_rope(jnp.concatenate([k_nope, kr], axis=0), kbg_ref[...])
        v = kvup[hd * per_head + QK_NOPE_DIM:(hd + 1) * per_head]
        vb_ref[hd // 2, (hd % 2) * V_DIM_B:(hd % 2 + 1) * V_DIM_B, :] = v.astype(BF16)


def _mix_proj(x, g, wa, wm, cqn, wuq, ckvn, wukv, qag, kag, qbg, kbg, cos_t, sin_t):
    t = x.shape[0]
    tm = TOKEN_TILE

    def row(w):
        return pl.BlockSpec((tm, w), lambda i: (i, 0))

    def cols(*lead):
        return pl.BlockSpec((*lead, tm), lambda i: (*(0,) * len(lead), i))

    qk_shape = jax.ShapeDtypeStruct((N_HEADS_B, QK_DIM_B, t), BF16)
    return pl.pallas_call(
        _mix_proj_kernel,
        grid=(t // tm,),
        in_specs=[row(D_MODEL), _const_spec((1, D_MODEL)),
                  _const_spec((D_MODEL, SWA_IN)), _const_spec((MLA_IN, D_MODEL)),
                  _const_spec((Q_LORA_RANK, 1)), _const_spec((N_HEADS_B * QK_DIM_B, Q_LORA_RANK)),
                  _const_spec((KV_LORA_RANK, 1)),
                  _const_spec((N_HEADS_B * (QK_NOPE_DIM + V_DIM_B), KV_LORA_RANK)),
                  _const_spec((1, LANES)), _const_spec((1, LANES)),
                  _const_spec((QK_DIM_B, 1)), _const_spec((QK_DIM_B, 1)),
                  cols(ROPE_HALF), cols(ROPE_HALF)],
        out_specs=[row(WIDTH_A), row(LANES), row(LANES),
                   cols(N_HEADS_B, QK_DIM_B), cols(N_HEADS_B, QK_DIM_B), cols(N_HEADS_B // 2, LANES)],
        out_shape=[jax.ShapeDtypeStruct((t, w), BF16) for w in (WIDTH_A, LANES, LANES)]
        + [qk_shape, qk_shape, jax.ShapeDtypeStruct((N_HEADS_B // 2, LANES, t), BF16)],
        compiler_params=_params(1),
        name="mix_proj",
    )(x, g, wa, wm, cqn, wuq, ckvn, wukv, qag, kag, qbg, kbg, cos_t, sin_t)


SWA_BLOCKS_PER_STEP = 8


def _swa_kernel(sink_ref, q_ref, k_ref, v_ref, pq_ref, pk_ref, o_ref):
    step = pl.program_id(1)
    blk = WINDOW
    lane = lax.broadcasted_iota(jnp.int32, (1, LANES), 1)
    lo = lane < HALF
    qi = lax.broadcasted_iota(jnp.int32, (blk, 2 * blk), 0) + blk
    ki = lax.broadcasted_iota(jnp.int32, (blk, 2 * blk), 1)
    diff = qi - ki
    band = (diff >= 0) & (diff < WINDOW)
    zero = jnp.zeros((), BF16)

    for r in range(SWA_BLOCKS_PER_STEP):
        cur = step * SWA_BLOCKS_PER_STEP + r
        rows = slice(r * blk, (r + 1) * blk)
        if r == 0:
            prev = jnp.maximum(cur - 1, 0)
            p0 = pl.multiple_of(prev * blk, blk)
            c0 = pl.multiple_of(cur * blk, blk)
            kk = jnp.concatenate([k_ref[0, pl.ds(p0, blk), :], k_ref[0, pl.ds(c0, blk), :]], axis=0)
            vv = jnp.concatenate([v_ref[0, pl.ds(p0, blk), :], v_ref[0, pl.ds(c0, blk), :]], axis=0)
            valid = band & ((ki >= blk) | (step > 0))
        else:
            prev = cur - 1
            p0 = pl.multiple_of(prev * blk, blk)
            kk = k_ref[0, pl.ds(p0, 2 * blk), :]
            vv = v_ref[0, pl.ds(p0, 2 * blk), :]
            valid = band
        pk = jnp.concatenate([pk_ref[0, prev], pk_ref[0, cur]], axis=1)
        dist = jnp.where(valid, (pq_ref[0, rows, :] - pk).astype(F32), MASK_DIST)

        o_kv, sink_kv = [], []
        for kv in range(N_KV_HEADS_A):
            keep = lo if kv == 0 else jnp.logical_not(lo)
            qs = [jnp.where(keep, q_ref[0, rows, j * LANES:(j + 1) * LANES], zero) for j in range(GROUP_A)]
            s_all = _dot_nt(jnp.concatenate(qs, axis=0), kk)
            ps, sink_terms = [], []
            for j in range(GROUP_A):
                hd = kv * GROUP_A + j
                slope = 2.0 ** (-8.0 * (hd + 1) / N_HEADS_A)
                sink = sink_ref[hd]
                s = s_all[j * blk:(j + 1) * blk] - slope * dist
                m = jnp.maximum(jnp.max(s, axis=-1, keepdims=True), sink)
                ps.append(jnp.exp(s - m).astype(BF16))
                sink_terms.append(jnp.exp(sink - m))
            v_ext = jnp.where(keep, vv, jnp.ones((), BF16))
            o_kv.append(_dot(jnp.concatenate(ps, axis=0), v_ext))
            sink_kv.append(sink_terms)
        for j in range(GROUP_A):
            o0 = o_kv[0][j * blk:(j + 1) * blk]
            o1 = o_kv[1][j * blk:(j + 1) * blk]
            num = jnp.where(lo, o0, o1)
            den = pltpu.roll(jnp.where(lo, o1, o0), HALF, 1) + jnp.where(lo, sink_kv[0][j], sink_kv[1][j])
            o_ref[0, rows, j * LANES:(j + 1) * LANES] = (num / den).astype(BF16)


def _swa_attn(sinks, qa, ka, va, pos_col, pos_blocks):
    b, s, _ = qa.shape
    rows = SWA_BLOCKS_PER_STEP * WINDOW
    cur = lambda bi, i: (bi, i, 0)
    whole = lambda bi, i: (bi, 0, 0)
    return pl.pallas_call(
        _swa_kernel,
        grid=(b, s // rows),
        in_specs=[pl.BlockSpec(memory_space=pltpu.SMEM),
                  pl.BlockSpec((1, rows, WIDTH_A), cur),
                  pl.BlockSpec((1, s, LANES), whole), pl.BlockSpec((1, s, LANES), whole),
                  pl.BlockSpec((1, rows, 1), cur),
                  pl.BlockSpec((1, s // WINDOW, 1, WINDOW), lambda bi, i: (bi, 0, 0, 0))],
        out_specs=pl.BlockSpec((1, rows, WIDTH_A), cur),
        out_shape=jax.ShapeDtypeStruct((b, s, WIDTH_A), BF16),
        compiler_params=_params(2),
        name="swa_attn",
    )(sinks, qa, ka, va, pos_col, pos_blocks)


def _mla_kernel(q_ref, k_ref, v_ref, o_ref):
    s_len = q_ref.shape[2]
    tq = ATTN_BLOCK_B
    c = QK_DIM_B ** -0.5 * np.log2(np.e)
    causal = (lax.broadcasted_iota(jnp.int32, (tq, tq), 0)
              >= lax.broadcasted_iota(jnp.int32, (tq, tq), 1))
    ones = jnp.ones((V_DIM_B, s_len), BF16)
    lo = lax.broadcasted_iota(jnp.int32, (1, LANES), 1) < HALF

    for pair in range(MLA_PAIRS_PER_STEP):
        v = v_ref[pair]
        v_ext = (jnp.concatenate([v[0:V_DIM_B], ones], axis=0),
                 jnp.concatenate([ones, v[V_DIM_B:]], axis=0))
        for i in reversed(range(s_len // tq)):
            n = (i + 1) * tq
            o = []
            for half in range(2):
                hd = 2 * pair + half
                q = q_ref[hd, :, i * tq:n].T
                s = _dot(q, k_ref[hd, :, 0:n])
                diag = jnp.where(causal, s[:, n - tq:], NEG)
                s = diag if i == 0 else jnp.concatenate([s[:, :n - tq], diag], axis=1)
                m = jnp.max(s, axis=-1, keepdims=True)
                p = jnp.exp2((s - m) * c).astype(BF16)
                o.append(_dot_nt(p, v_ext[half][:, 0:n]))
            num = jnp.where(lo, o[0], o[1])
            den = pltpu.roll(jnp.where(lo, o[1], o[0]), HALF, 1)
            o_ref[0, i * tq:n, pair * LANES:(pair + 1) * LANES] = (num / den).astype(BF16)


def _mla_attn(qb, kb, vb, b, s):
    pp = MLA_PAIRS_PER_STEP
    qk_spec = pl.BlockSpec((2 * pp, QK_DIM_B, s), lambda bi, j: (j, 0, bi))
    return pl.pallas_call(
        _mla_kernel,
        grid=(b, N_HEADS_B // (2 * pp)),
        in_specs=[qk_spec, qk_spec, pl.BlockSpec((pp, LANES, s), lambda bi, j: (j, 0, bi))],
        out_specs=pl.BlockSpec((1, s, pp * LANES), lambda bi, j: (bi, 0, j)),
        out_shape=jax.ShapeDtypeStruct((b, s, WIDTH_B), BF16),
        compiler_params=_params(2),
        name="mla_attn",
    )(qb, kb, vb)


def _merge_kernel(x_ref, g_ref, oa_ref, ob_ref, wg_ref, wba_ref, wbb_ref, wo_ref, o_ref):
    x = x_ref[...]
    h = _rms(x, g_ref[...], D_MODEL).astype(BF16)
    gates = _dot(h, wg_ref[...])
    ma = _dot(oa_ref[...], wba_ref[...])
    mb = _dot(ob_ref[...], wbb_ref[...])
    merged = jax.nn.sigmoid(gates[:, :D_MODEL]) * ma + jax.nn.sigmoid(gates[:, D_MODEL:]) * mb
    o_ref[...] = x + _dot(merged.astype(BF16), wo_ref[...])


def _merge(x, g, oa, ob, wg, wba, wbb, wo):
    t = x.shape[0]
    tm = TOKEN_TILE

    def row(w):
        return pl.BlockSpec((tm, w), lambda i: (i, 0))

    return pl.pallas_call(
        _merge_kernel,
        grid=(t // tm,),
        in_specs=[row(D_MODEL), _const_spec((1, D_MODEL)), row(WIDTH_A), row(WIDTH_B),
                  _const_spec((D_MODEL, 2 * D_MODEL)), _const_spec((WIDTH_A, D_MODEL)),
                  _const_spec((WIDTH_B, D_MODEL)), _const_spec((D_MODEL, D_MODEL))],
        out_specs=row(D_MODEL),
        out_shape=jax.ShapeDtypeStruct((t, D_MODEL), F32),
        compiler_params=_params(1),
        name="merge",
    )(x, g, oa, ob, wg, wba, wbb, wo)


def kernel(x, positions, ffn1_norm, ffn1_w_gate, ffn1_w_up, ffn1_w_down, mix_norm, w_in, swa_q_norm, swa_k_norm, swa_sinks, mla_q_lora_norm, mla_w_uq, mla_kv_lora_norm, mla_w_ukv, mla_q_norm, mla_k_norm, w_branch_a, w_branch_b, w_out, ffn2_norm, ffn2_w_gate, ffn2_w_up, ffn2_w_down):
    b, s, d = x.shape
    t = b * s
    depth = w_in.shape[0]
    assert d == D_MODEL and s % ATTN_BLOCK_B == 0 and t % TOKEN_TILE == 0 and t % min(ROPE_TILE, t) == 0
    assert s % (SWA_BLOCKS_PER_STEP * WINDOW) == 0

    cos_t, sin_t = _rope_tables(positions.reshape(1, t))
    pos_col3 = positions.reshape(b, s, 1)
    pos_blocks = positions.reshape(b, s // WINDOW, 1, WINDOW)

    ffn_f32 = []
    for l in range(depth):
        ffn_f32.append((ffn1_w_gate[l], ffn1_w_up[l], ffn1_w_down[l]))
        ffn_f32.append((ffn2_w_gate[l], ffn2_w_up[l], ffn2_w_down[l]))
    ffn_f32.append(())
    ffn_w = tuple(w.astype(BF16) for w in ffn_f32[0])

    xt = x.reshape(t, d)
    for l in range(depth):
        xt, ffn_w = _ffn(xt, ffn1_norm[l].reshape(1, d), *ffn_w, next_weights=ffn_f32[2 * l + 1])

        wi = w_in[l]
        wa = jnp.concatenate([jnp.take(wi, OFF_QA + _SWA_COLS, axis=1), wi[:, OFF_KA:OFF_CQ]],
                             axis=1).astype(BF16)
        wm = wi[:, OFF_CQ:OFF_GA].T.astype(BF16)
        qag = (jnp.tile(swa_q_norm[l], 2) * HEAD_DIM_A ** -0.5).reshape(1, LANES)
        kag = jnp.tile(swa_k_norm[l], 2).reshape(1, LANES)
        qa, ka, va, qb, kb, vb = _mix_proj(
            xt, mix_norm[l].reshape(1, d), wa, wm,
            mla_q_lora_norm[l].reshape(-1, 1), mla_w_uq[l].T.astype(BF16),
            mla_kv_lora_norm[l].reshape(-1, 1), mla_w_ukv[l].T.astype(BF16), qag, kag,
            mla_q_norm[l].reshape(-1, 1), mla_k_norm[l].reshape(-1, 1), cos_t, sin_t)

        oa = _swa_attn(swa_sinks[l], qa.reshape(b, s, -1), ka.reshape(b, s, -1), va.reshape(b, s, -1),
                       pos_col3, pos_blocks)
        ob = _mla_attn(qb, kb, vb, b, s)

        xt = _merge(xt, mix_norm[l].reshape(1, d), oa.reshape(t, -1), ob.reshape(t, -1),
                    wi[:, OFF_GA:].astype(BF16), jnp.take(w_branch_a[l], _SWA_COLS, axis=0).astype(BF16),
                    w_branch_b[l].astype(BF16), w_out[l].astype(BF16))

        xt, ffn_w = _ffn(xt, ffn2_norm[l].reshape(1, d), *ffn_w, next_weights=ffn_f32[2 * l + 2])
    return xt.reshape(b, s, d)
```

```python
import numpy as np
import jax
import jax.numpy as jnp
from jax import lax
from jax.experimental import pallas as pl
from jax.experimental.pallas import tpu as pltpu

F32 = jnp.float32
BF16 = jnp.bfloat16

D_MODEL = 1024
D_FF = 2816
HEAD_DIM_A = 64
N_HEADS_A = 8
N_KV_HEADS_A = 2
GROUP_A = N_HEADS_A // N_KV_HEADS_A
WINDOW = 128
N_HEADS_B = 8
Q_LORA_RANK = 256
KV_LORA_RANK = 128
QK_NOPE_DIM = 64
QK_ROPE_DIM = 32
QK_DIM_B = QK_NOPE_DIM + QK_ROPE_DIM
V_DIM_B = 64
ROPE_BASE = 10000.0
WIDTH_A = N_HEADS_A * HEAD_DIM_A
WIDTH_B = N_HEADS_B * V_DIM_B
EPS = 1e-6
NEG = -1e30
MASK_DIST = 1e30

LANES = 128
HALF = LANES // 2
BF16_SUBLANES = 16

OFF_QA = 0
OFF_KA = OFF_QA + WIDTH_A
OFF_VA = OFF_KA + N_KV_HEADS_A * HEAD_DIM_A
OFF_CQ = OFF_VA + N_KV_HEADS_A * HEAD_DIM_A
OFF_CKV = OFF_CQ + Q_LORA_RANK
OFF_KR = OFF_CKV + KV_LORA_RANK
OFF_GA = OFF_KR + QK_ROPE_DIM
OFF_GB = OFF_GA + D_MODEL
IN_WIDTH = OFF_GB + D_MODEL

TOKEN_TILE = 1024
FF_CHUNK = 256
ATTN_BLOCK_B = 512
MLA_PAIRS_PER_STEP = 2
ROPE_TILE = 4096
VMEM_LIMIT = 56 * 1024 * 1024

ROPE_HALF = QK_ROPE_DIM // 2
SWA_IN = OFF_CQ
MLA_IN = Q_LORA_RANK + KV_LORA_RANK + QK_ROPE_DIM

_SWA_HEAD_ORDER = np.array([h for j in range(GROUP_A) for h in (j, j + GROUP_A)])
_SWA_COLS = (_SWA_HEAD_ORDER[:, None] * HEAD_DIM_A + np.arange(HEAD_DIM_A)[None, :]).reshape(-1)


def _rms(x, g, n):
    ssq = jnp.sum(x * x, axis=-1, keepdims=True)
    return x * lax.rsqrt(ssq / n + EPS) * g


def _rms_rows(c, g, n):
    ssq = jnp.sum(c * c, axis=0, keepdims=True)
    return c * lax.rsqrt(ssq / n + EPS) * g


def _dot(a, b):
    return jnp.dot(a, b, preferred_element_type=F32)


def _dot_nt(a, b):
    return lax.dot_general(a, b, (((1,), (1,)), ((), ())), preferred_element_type=F32)


def _const_spec(shape):
    return pl.BlockSpec(shape, lambda *_: (0,) * len(shape), pipeline_mode=pl.Buffered(1))


def _params(n_axes):
    return pltpu.CompilerParams(dimension_semantics=("parallel",) * n_axes,
                                vmem_limit_bytes=VMEM_LIMIT)


def _ffn_kernel(x_ref, g_ref, wg_ref, wu_ref, wd_ref, *rest):
    o_ref = rest[len(rest) // 2]
    x = x_ref[...]
    h = _rms(x, g_ref[...], D_MODEL).astype(BF16)
    y = None
    for c in range(D_FF // FF_CHUNK):
        sl = slice(c * FF_CHUNK, (c + 1) * FF_CHUNK)
        gate = _dot(h, wg_ref[:, sl])
        up = _dot(h, wu_ref[:, sl])
        act = (gate * jax.nn.sigmoid(gate) * up).astype(BF16)
        d = _dot(act, wd_ref[sl, :])
        y = d if y is None else y + d
    o_ref[...] = x + 0.5 * y
    for src_ref, dst_ref in zip(rest[:len(rest) // 2], rest[len(rest) // 2 + 1:]):
        dst_ref[...] = src_ref[...].astype(BF16)


def _slab_specs(rows, cols, n_steps, layer):
    blk = next(r for r in range(BF16_SUBLANES, rows + 1, BF16_SUBLANES)
               if rows % r == 0 and rows // r <= n_steps)
    last = rows // blk - 1
    return (pl.BlockSpec((None, blk, cols), lambda i: (layer, jnp.minimum(i, last), 0)),
            pl.BlockSpec((blk, cols), lambda i: (jnp.minimum(i, last), 0)))


def _ffn(x, g, wg, wu, wd, next_weights=(), next_layer=0):
    t = x.shape[0]
    tm = TOKEN_TILE
    n_steps = t // tm
    row = pl.BlockSpec((tm, D_MODEL), lambda i: (i, 0))
    specs = [_slab_specs(*w.shape[1:], n_steps, next_layer) for w in next_weights]
    slabs_in = [s[0] for s in specs]
    slabs = [s[1] for s in specs]
    out = pl.pallas_call(
        _ffn_kernel,
        grid=(n_steps,),
        in_specs=[row, _const_spec((1, D_MODEL)), _const_spec((D_MODEL, D_FF)),
                  _const_spec((D_MODEL, D_FF)), _const_spec((D_FF, D_MODEL))] + slabs_in,
        out_specs=[row] + slabs,
        out_shape=[jax.ShapeDtypeStruct((t, D_MODEL), F32)]
        + [jax.ShapeDtypeStruct(w.shape[1:], BF16) for w in next_weights],
        compiler_params=pltpu.CompilerParams(dimension_semantics=("arbitrary",),
                                             vmem_limit_bytes=VMEM_LIMIT),
        name="ffn",
    )(x, g, wg, wu, wd, *next_weights)
    return out[0], tuple(out[1:])


def _rope_kernel(pos_ref, invf_ref, c_ref, s_ref):
    ang = pos_ref[...].astype(F32) * invf_ref[...]
    c_ref[...] = jnp.cos(ang)
    s_ref[...] = jnp.sin(ang)


def _rope_tables(pos_row):
    t = pos_row.shape[1]
    tm = min(ROPE_TILE, t)
    inv_freq = ROPE_BASE ** (-jnp.arange(ROPE_HALF, dtype=F32) / ROPE_HALF)
    out = jax.ShapeDtypeStruct((ROPE_HALF, t), F32)
    return pl.pallas_call(
        _rope_kernel,
        grid=(t // tm,),
        in_specs=[pl.BlockSpec((1, tm), lambda i: (0, i)), _const_spec((ROPE_HALF, 1))],
        out_specs=[pl.BlockSpec((ROPE_HALF, tm), lambda i: (0, i))] * 2,
        out_shape=[out, out],
        compiler_params=_params(1),
        name="rope_tables",
    )(pos_row, inv_freq.reshape(ROPE_HALF, 1))


def _mix_proj_kernel(x_ref, g_ref, wa_ref, wm_ref, cqn_ref, wuq_ref, ckvn_ref, wukv_ref,
                     qag_ref, kag_ref, qbg_ref, kbg_ref, cos_ref, sin_ref,
                     qa_ref, ka_ref, va_ref, qb_ref, kb_ref, vb_ref):
    x = x_ref[...]
    h = _rms(x, g_ref[...], D_MODEL).astype(BF16)

    pa = _dot(h, wa_ref[...])
    lane = lax.broadcasted_iota(jnp.int32, (1, LANES), 1)
    lo = lane < HALF

    def half_norm(c, gain):
        sq = c * c
        s_lo = jnp.sum(jnp.where(lo, sq, 0.0), axis=-1, keepdims=True)
        s_hi = jnp.sum(jnp.where(lo, 0.0, sq), axis=-1, keepdims=True)
        inv = jnp.where(lo, lax.rsqrt(s_lo / HEAD_DIM_A + EPS), lax.rsqrt(s_hi / HEAD_DIM_A + EPS))
        return c * inv * gain

    for j in range(WIDTH_A // LANES):
        c = pa[:, OFF_QA + j * LANES: OFF_QA + (j + 1) * LANES]
        qa_ref[:, j * LANES:(j + 1) * LANES] = half_norm(c, qag_ref[...]).astype(BF16)
    ka_ref[...] = half_norm(pa[:, OFF_KA:OFF_VA], kag_ref[...]).astype(BF16)
    va_ref[...] = pa[:, OFF_VA:OFF_CQ].astype(BF16)

    pm = _dot_nt(wm_ref[...], h)
    cq = _rms_rows(pm[0:Q_LORA_RANK], cqn_ref[...], Q_LORA_RANK).astype(BF16)
    ckv = _rms_rows(pm[Q_LORA_RANK:Q_LORA_RANK + KV_LORA_RANK], ckvn_ref[...], KV_LORA_RANK).astype(BF16)
    kr = pm[Q_LORA_RANK + KV_LORA_RANK:MLA_IN]
    qup = _dot(wuq_ref[...], cq)
    kvup = _dot(wukv_ref[...], ckv)
    cos = cos_ref[...]
    sin = sin_ref[...]

    def norm_rope(c, gain):
        y = _rms_rows(c, gain, QK_DIM_B)
        y1 = y[QK_NOPE_DIM:QK_NOPE_DIM + ROPE_HALF]
        y2 = y[QK_NOPE_DIM + ROPE_HALF:QK_DIM_B]
        return jnp.concatenate([y[0:QK_NOPE_DIM], y1 * cos - y2 * sin, y2 * cos + y1 * sin],
                               axis=0).astype(BF16)

    per_head = QK_NOPE_DIM + V_DIM_B
    for hd in range(N_HEADS_B):
        qb_ref[hd] = norm_rope(qup[hd * QK_DIM_B:(hd + 1) * QK_DIM_B], qbg_ref[...])
        k_nope = kvup[hd * per_head:hd * per_head + QK_NOPE_DIM]
        kb_ref[hd] = norm_rope(jnp.concatenate([k_nope, kr], axis=0), kbg_ref[...])
        v = kvup[hd * per_head + QK_NOPE_DIM:(hd + 1) * per_head]
        vb_ref[hd // 2, (hd % 2) * V_DIM_B:(hd % 2 + 1) * V_DIM_B, :] = v.astype(BF16)


def _mix_proj(x, g, wa, wm, cqn, wuq, ckvn, wukv, qag, kag, qbg, kbg, cos_t, sin_t):
    t = x.shape[0]
    tm = TOKEN_TILE

    def row(w):
        return pl.BlockSpec((tm, w), lambda i: (i, 0))

    def cols(*lead):
        return pl.BlockSpec((*lead, tm), lambda i: (*(0,) * len(lead), i))

    qk_shape = jax.ShapeDtypeStruct((N_HEADS_B, QK_DIM_B, t), BF16)
    return pl.pallas_call(
        _mix_proj_kernel,
        grid=(t // tm,),
        in_specs=[row(D_MODEL), _const_spec((1, D_MODEL)),
                  _const_spec((D_MODEL, SWA_IN)), _const_spec((MLA_IN, D_MODEL)),
                  _const_spec((Q_LORA_RANK, 1)), _const_spec((N_HEADS_B * QK_DIM_B, Q_LORA_RANK)),
                  _const_spec((KV_LORA_RANK, 1)),
                  _const_spec((N_HEADS_B * (QK_NOPE_DIM + V_DIM_B), KV_LORA_RANK)),
                  _const_spec((1, LANES)), _const_spec((1, LANES)),
                  _const_spec((QK_DIM_B, 1)), _const_spec((QK_DIM_B, 1)),
                  cols(ROPE_HALF), cols(ROPE_HALF)],
        out_specs=[row(WIDTH_A), row(LANES), row(LANES),
                   cols(N_HEADS_B, QK_DIM_B), cols(N_HEADS_B, QK_DIM_B), cols(N_HEADS_B // 2, LANES)],
        out_shape=[jax.ShapeDtypeStruct((t, w), BF16) for w in (WIDTH_A, LANES, LANES)]
        + [qk_shape, qk_shape, jax.ShapeDtypeStruct((N_HEADS_B // 2, LANES, t), BF16)],
        compiler_params=_params(1),
        name="mix_proj",
    )(x, g, wa, wm, cqn, wuq, ckvn, wukv, qag, kag, qbg, kbg, cos_t, sin_t)


SWA_BLOCKS_PER_STEP = 8


def _swa_kernel(sink_ref, q_ref, k_ref, v_ref, pq_ref, pk_ref, o_ref):
    step = pl.program_id(1)
    blk = WINDOW
    lane = lax.broadcasted_iota(jnp.int32, (1, LANES), 1)
    lo = lane < HALF
    qi = lax.broadcasted_iota(jnp.int32, (blk, 2 * blk), 0) + blk
    ki = lax.broadcasted_iota(jnp.int32, (blk, 2 * blk), 1)
    diff = qi - ki
    band = (diff >= 0) & (diff < WINDOW)
    zero = jnp.zeros((), BF16)

    for r in range(SWA_BLOCKS_PER_STEP):
        cur = step * SWA_BLOCKS_PER_STEP + r
        rows = slice(r * blk, (r + 1) * blk)
        if r == 0:
            prev = jnp.maximum(cur - 1, 0)
            p0 = pl.multiple_of(prev * blk, blk)
            c0 = pl.multiple_of(cur * blk, blk)
            kk = jnp.concatenate([k_ref[0, pl.ds(p0, blk), :], k_ref[0, pl.ds(c0, blk), :]], axis=0)
            vv = jnp.concatenate([v_ref[0, pl.ds(p0, blk), :], v_ref[0, pl.ds(c0, blk), :]], axis=0)
            valid = band & ((ki >= blk) | (step > 0))
        else:
            prev = cur - 1
            p0 = pl.multiple_of(prev * blk, blk)
            kk = k_ref[0, pl.ds(p0, 2 * blk), :]
            vv = v_ref[0, pl.ds(p0, 2 * blk), :]
            valid = band
        pk = jnp.concatenate([pk_ref[0, prev], pk_ref[0, cur]], axis=1)
        dist = jnp.where(valid, (pq_ref[0, rows, :] - pk).astype(F32), MASK_DIST)

        o_kv, sink_kv = [], []
        for kv in range(N_KV_HEADS_A):
            keep = lo if kv == 0 else jnp.logical_not(lo)
            qs = [jnp.where(keep, q_ref[0, rows, j * LANES:(j + 1) * LANES], zero) for j in range(GROUP_A)]
            s_all = _dot_nt(jnp.concatenate(qs, axis=0), kk)
            ps, sink_terms = [], []
            for j in range(GROUP_A):
                hd = kv * GROUP_A + j
                slope = 2.0 ** (-8.0 * (hd + 1) / N_HEADS_A)
                sink = sink_ref[hd]
                s = s_all[j * blk:(j + 1) * blk] - slope * dist
                m = jnp.maximum(jnp.max(s, axis=-1, keepdims=True), sink)
                ps.append(jnp.exp(s - m).astype(BF16))
                sink_terms.append(jnp.exp(sink - m))
            v_ext = jnp.where(keep, vv, jnp.ones((), BF16))
            o_kv.append(_dot(jnp.concatenate(ps, axis=0), v_ext))
            sink_kv.append(sink_terms)
        for j in range(GROUP_A):
            o0 = o_kv[0][j * blk:(j + 1) * blk]
            o1 = o_kv[1][j * blk:(j + 1) * blk]
            num = jnp.where(lo, o0, o1)
            den = pltpu.roll(jnp.where(lo, o1, o0), HALF, 1) + jnp.where(lo, sink_kv[0][j], sink_kv[1][j])
            o_ref[0, rows, j * LANES:(j + 1) * LANES] = (num / den).astype(BF16)


def _swa_attn(sinks, qa, ka, va, pos_col, pos_blocks):
    b, s, _ = qa.shape
    rows = SWA_BLOCKS_PER_STEP * WINDOW
    cur = lambda bi, i: (bi, i, 0)
    whole = lambda bi, i: (bi, 0, 0)
    return pl.pallas_call(
        _swa_kernel,
        grid=(b, s // rows),
        in_specs=[pl.BlockSpec(memory_space=pltpu.SMEM),
                  pl.BlockSpec((1, rows, WIDTH_A), cur),
                  pl.BlockSpec((1, s, LANES), whole), pl.BlockSpec((1, s, LANES), whole),
                  pl.BlockSpec((1, rows, 1), cur),
                  pl.BlockSpec((1, s // WINDOW, 1, WINDOW), lambda bi, i: (bi, 0, 0, 0))],
        out_specs=pl.BlockSpec((1, rows, WIDTH_A), cur),
        out_shape=jax.ShapeDtypeStruct((b, s, WIDTH_A), BF16),
        compiler_params=_params(2),
        name="swa_attn",
    )(sinks, qa, ka, va, pos_col, pos_blocks)


def _mla_kernel(q_ref, k_ref, v_ref, o_ref):
    s_len = q_ref.shape[2]
    tq = ATTN_BLOCK_B
    c = QK_DIM_B ** -0.5 * np.log2(np.e)
    causal = (lax.broadcasted_iota(jnp.int32, (tq, tq), 0)
              >= lax.broadcasted_iota(jnp.int32, (tq, tq), 1))
    ones = jnp.ones((V_DIM_B, s_len), BF16)
    lo = lax.broadcasted_iota(jnp.int32, (1, LANES), 1) < HALF

    for pair in range(MLA_PAIRS_PER_STEP):
        v = v_ref[pair]
        v_ext = (jnp.concatenate([v[0:V_DIM_B], ones], axis=0),
                 jnp.concatenate([ones, v[V_DIM_B:]], axis=0))
        for i in reversed(range(s_len // tq)):
            n = (i + 1) * tq
            o = []
            for half in range(2):
                hd = 2 * pair + half
                q = q_ref[hd, :, i * tq:n].T
                s = _dot(q, k_ref[hd, :, 0:n])
                diag = jnp.where(causal, s[:, n - tq:], NEG)
                s = diag if i == 0 else jnp.concatenate([s[:, :n - tq], diag], axis=1)
                m = jnp.max(s, axis=-1, keepdims=True)
                p = jnp.exp2((s - m) * c).astype(BF16)
                o.append(_dot_nt(p, v_ext[half][:, 0:n]))
            num = jnp.where(lo, o[0], o[1])
            den = pltpu.roll(jnp.where(lo, o[1], o[0]), HALF, 1)
            o_ref[0, i * tq:n, pair * LANES:(pair + 1) * LANES] = (num / den).astype(BF16)


def _mla_attn(qb, kb, vb, b, s):
    pp = MLA_PAIRS_PER_STEP
    qk_spec = pl.BlockSpec((2 * pp, QK_DIM_B, s), lambda bi, j: (j, 0, bi))
    return pl.pallas_call(
        _mla_kernel,
        grid=(b, N_HEADS_B // (2 * pp)),
        in_specs=[qk_spec, qk_spec, pl.BlockSpec((pp, LANES, s), lambda bi, j: (j, 0, bi))],
        out_specs=pl.BlockSpec((1, s, pp * LANES), lambda bi, j: (bi, 0, j)),
        out_shape=jax.ShapeDtypeStruct((b, s, WIDTH_B), BF16),
        compiler_params=_params(2),
        name="mla_attn",
    )(qb, kb, vb)


def _merge_kernel(x_ref, g_ref, oa_ref, ob_ref, wg_ref, wba_ref, wbb_ref, wo_ref, o_ref):
    x = x_ref[...]
    h = _rms(x, g_ref[...], D_MODEL).astype(BF16)
    gates = _dot(h, wg_ref[...])
    ma = _dot(oa_ref[...], wba_ref[...])
    mb = _dot(ob_ref[...], wbb_ref[...])
    merged = jax.nn.sigmoid(gates[:, :D_MODEL]) * ma + jax.nn.sigmoid(gates[:, D_MODEL:]) * mb
    o_ref[...] = x + _dot(merged.astype(BF16), wo_ref[...])


def _merge(x, g, oa, ob, wg, wba, wbb, wo):
    t = x.shape[0]
    tm = TOKEN_TILE

    def row(w):
        return pl.BlockSpec((tm, w), lambda i: (i, 0))

    return pl.pallas_call(
        _merge_kernel,
        grid=(t // tm,),
        in_specs=[row(D_MODEL), _const_spec((1, D_MODEL)), row(WIDTH_A), row(WIDTH_B),
                  _const_spec((D_MODEL, 2 * D_MODEL)), _const_spec((WIDTH_A, D_MODEL)),
                  _const_spec((WIDTH_B, D_MODEL)), _const_spec((D_MODEL, D_MODEL))],
        out_specs=row(D_MODEL),
        out_shape=jax.ShapeDtypeStruct((t, D_MODEL), F32),
        compiler_params=_params(1),
        name="merge",
    )(x, g, oa, ob, wg, wba, wbb, wo)


def kernel(x, positions, ffn1_norm, ffn1_w_gate, ffn1_w_up, ffn1_w_down, mix_norm, w_in, swa_q_norm, swa_k_norm, swa_sinks, mla_q_lora_norm, mla_w_uq, mla_kv_lora_norm, mla_w_ukv, mla_q_norm, mla_k_norm, w_branch_a, w_branch_b, w_out, ffn2_norm, ffn2_w_gate, ffn2_w_up, ffn2_w_down):
    b, s, d = x.shape
    t = b * s
    depth = w_in.shape[0]
    assert d == D_MODEL and s % ATTN_BLOCK_B == 0 and t % TOKEN_TILE == 0 and t % min(ROPE_TILE, t) == 0
    assert s % (SWA_BLOCKS_PER_STEP * WINDOW) == 0

    cos_t, sin_t = _rope_tables(positions.reshape(1, t))
    pos_col3 = positions.reshape(b, s, 1)
    pos_blocks = positions.reshape(b, s // WINDOW, 1, WINDOW)

    ffn1_stack = (ffn1_w_gate, ffn1_w_up, ffn1_w_down)
    ffn2_stack = (ffn2_w_gate, ffn2_w_up, ffn2_w_down)
    ffn_w = tuple(w[0].astype(BF16) for w in ffn1_stack)

    xt = x.reshape(t, d)
    for l in range(depth):
        xt, ffn_w = _ffn(xt, ffn1_norm[l].reshape(1, d), *ffn_w, next_weights=ffn2_stack, next_layer=l)

        wi = w_in[l]
        wa = jnp.concatenate([jnp.take(wi, OFF_QA + _SWA_COLS, axis=1), wi[:, OFF_KA:OFF_CQ]],
                             axis=1).astype(BF16)
        wm = wi[:, OFF_CQ:OFF_GA].T.astype(BF16)
        qag = (jnp.tile(swa_q_norm[l], 2) * HEAD_DIM_A ** -0.5).reshape(1, LANES)
        kag = jnp.tile(swa_k_norm[l], 2).reshape(1, LANES)
        qa, ka, va, qb, kb, vb = _mix_proj(
            xt, mix_norm[l].reshape(1, d), wa, wm,
            mla_q_lora_norm[l].reshape(-1, 1), mla_w_uq[l].T.astype(BF16),
            mla_kv_lora_norm[l].reshape(-1, 1), mla_w_ukv[l].T.astype(BF16), qag, kag,
            mla_q_norm[l].reshape(-1, 1), mla_k_norm[l].reshape(-1, 1), cos_t, sin_t)

        oa = _swa_attn(swa_sinks[l], qa.reshape(b, s, -1), ka.reshape(b, s, -1), va.reshape(b, s, -1),
                       pos_col3, pos_blocks)
        ob = _mla_attn(qb, kb, vb, b, s)

        xt = _merge(xt, mix_norm[l].reshape(1, d), oa.reshape(t, -1), ob.reshape(t, -1),
                    wi[:, OFF_GA:].astype(BF16), jnp.take(w_branch_a[l], _SWA_COLS, axis=0).astype(BF16),
                    w_branch_b[l].astype(BF16), w_out[l].astype(BF16))

        last = l + 1 == depth
        xt, ffn_w = _ffn(xt, ffn2_norm[l].reshape(1, d), *ffn_w,
                         next_weights=() if last else ffn1_stack, next_layer=l + 1)
    return xt.reshape(b, s, d)
```

```python
import functools

import numpy as np
import jax
import jax.numpy as jnp
from jax import lax
from jax.experimental import pallas as pl
from jax.experimental.pallas import tpu as pltpu

F32 = jnp.float32
BF16 = jnp.bfloat16

D_MODEL = 1024
D_FF = 2816
HEAD_DIM_A = 64
N_HEADS_A = 8
N_KV_HEADS_A = 2
GROUP_A = N_HEADS_A // N_KV_HEADS_A
WINDOW = 128
N_HEADS_B = 8
Q_LORA_RANK = 256
KV_LORA_RANK = 128
QK_NOPE_DIM = 64
QK_ROPE_DIM = 32
QK_DIM_B = QK_NOPE_DIM + QK_ROPE_DIM
V_DIM_B = 64
ROPE_BASE = 10000.0
WIDTH_A = N_HEADS_A * HEAD_DIM_A
WIDTH_B = N_HEADS_B * V_DIM_B
EPS = 1e-6
NEG = -1e30
MASK_DIST = 1e30

LANES = 128
HALF = LANES // 2
BF16_SUBLANES = 16

OFF_QA = 0
OFF_KA = OFF_QA + WIDTH_A
OFF_VA = OFF_KA + N_KV_HEADS_A * HEAD_DIM_A
OFF_CQ = OFF_VA + N_KV_HEADS_A * HEAD_DIM_A
OFF_CKV = OFF_CQ + Q_LORA_RANK
OFF_KR = OFF_CKV + KV_LORA_RANK
OFF_GA = OFF_KR + QK_ROPE_DIM
OFF_GB = OFF_GA + D_MODEL
IN_WIDTH = OFF_GB + D_MODEL

TOKEN_TILE = 1024
FF_CHUNK = 256
ATTN_BLOCK_B = 512
MLA_PAIRS_PER_STEP = 2
ROPE_TILE = 4096
VMEM_LIMIT = 56 * 1024 * 1024

ROPE_HALF = QK_ROPE_DIM // 2
SWA_IN = OFF_CQ
MLA_IN = Q_LORA_RANK + KV_LORA_RANK + QK_ROPE_DIM

_SWA_HEAD_ORDER = np.array([h for j in range(GROUP_A) for h in (j, j + GROUP_A)])
_SWA_COLS = (_SWA_HEAD_ORDER[:, None] * HEAD_DIM_A + np.arange(HEAD_DIM_A)[None, :]).reshape(-1)


def _rms(x, g, n):
    ssq = jnp.sum(x * x, axis=-1, keepdims=True)
    return x * lax.rsqrt(ssq / n + EPS) * g


def _rms_rows(c, g, n):
    ssq = jnp.sum(c * c, axis=0, keepdims=True)
    return c * lax.rsqrt(ssq / n + EPS) * g


def _dot(a, b):
    return jnp.dot(a, b, preferred_element_type=F32)


def _dot_nt(a, b):
    return lax.dot_general(a, b, (((1,), (1,)), ((), ())), preferred_element_type=F32)


def _const_spec(shape):
    return pl.BlockSpec(shape, lambda *_: (0,) * len(shape), pipeline_mode=pl.Buffered(1))


def _params(n_axes):
    return pltpu.CompilerParams(dimension_semantics=("parallel",) * n_axes,
                                vmem_limit_bytes=VMEM_LIMIT)


def _ffn_kernel(x_ref, g_ref, wg_ref, wu_ref, wd_ref, *rest):
    o_ref = rest[len(rest) // 2]
    x = x_ref[...]
    h = _rms(x, g_ref[...], D_MODEL).astype(BF16)
    y = None
    for c in range(D_FF // FF_CHUNK):
        sl = slice(c * FF_CHUNK, (c + 1) * FF_CHUNK)
        gate = _dot(h, wg_ref[:, sl])
        up = _dot(h, wu_ref[:, sl])
        act = (gate * jax.nn.sigmoid(gate) * up).astype(BF16)
        d = _dot(act, wd_ref[sl, :])
        y = d if y is None else y + d
    o_ref[...] = x + 0.5 * y
    for src_ref, dst_ref in zip(rest[:len(rest) // 2], rest[len(rest) // 2 + 1:]):
        dst_ref[...] = src_ref[...].astype(BF16)


def _slab_specs(rows, cols, n_steps, layer):
    blk = next(r for r in range(BF16_SUBLANES, rows + 1, BF16_SUBLANES)
               if rows % r == 0 and rows // r <= n_steps)
    last = rows // blk - 1
    return (pl.BlockSpec((None, blk, cols), lambda i: (layer, jnp.minimum(i, last), 0)),
            pl.BlockSpec((blk, cols), lambda i: (jnp.minimum(i, last), 0)))


def _ffn(x, g, wg, wu, wd, next_weights=(), next_layer=0):
    t = x.shape[0]
    tm = TOKEN_TILE
    n_steps = t // tm
    row = pl.BlockSpec((tm, D_MODEL), lambda i: (i, 0))
    specs = [_slab_specs(*w.shape[1:], n_steps, next_layer) for w in next_weights]
    slabs_in = [s[0] for s in specs]
    slabs = [s[1] for s in specs]
    out = pl.pallas_call(
        _ffn_kernel,
        grid=(n_steps,),
        in_specs=[row, _const_spec((1, D_MODEL)), _const_spec((D_MODEL, D_FF)),
                  _const_spec((D_MODEL, D_FF)), _const_spec((D_FF, D_MODEL))] + slabs_in,
        out_specs=[row] + slabs,
        out_shape=[jax.ShapeDtypeStruct((t, D_MODEL), F32)]
        + [jax.ShapeDtypeStruct(w.shape[1:], BF16) for w in next_weights],
        compiler_params=pltpu.CompilerParams(dimension_semantics=("arbitrary",),
                                             vmem_limit_bytes=VMEM_LIMIT),
        name="ffn",
    )(x, g, wg, wu, wd, *next_weights)
    return out[0], tuple(out[1:])


def _rope_kernel(pos_ref, invf_ref, c_ref, s_ref):
    ang = pos_ref[...].astype(F32) * invf_ref[...]
    c_ref[...] = jnp.cos(ang)
    s_ref[...] = jnp.sin(ang)


def _rope_tables(pos_row):
    t = pos_row.shape[1]
    tm = min(ROPE_TILE, t)
    inv_freq = ROPE_BASE ** (-jnp.arange(ROPE_HALF, dtype=F32) / ROPE_HALF)
    out = jax.ShapeDtypeStruct((ROPE_HALF, t), F32)
    return pl.pallas_call(
        _rope_kernel,
        grid=(t // tm,),
        in_specs=[pl.BlockSpec((1, tm), lambda i: (0, i)), _const_spec((ROPE_HALF, 1))],
        out_specs=[pl.BlockSpec((ROPE_HALF, tm), lambda i: (0, i))] * 2,
        out_shape=[out, out],
        compiler_params=_params(1),
        name="rope_tables",
    )(pos_row, inv_freq.reshape(ROPE_HALF, 1))


def _mix_proj_kernel(x_ref, g_ref, wa_ref, wm_ref, cqn_ref, wuq_ref, ckvn_ref, wukv_ref,
                     qag_ref, kag_ref, qbg_ref, kbg_ref, cos_ref, sin_ref,
                     qa_ref, ka_ref, va_ref, qb_ref, kb_ref, vb_ref):
    x = x_ref[...]
    h = _rms(x, g_ref[...], D_MODEL).astype(BF16)

    pa = _dot(h, wa_ref[...])
    lane = lax.broadcasted_iota(jnp.int32, (1, LANES), 1)
    lo = lane < HALF

    def half_norm(c, gain):
        sq = c * c
        s_lo = jnp.sum(jnp.where(lo, sq, 0.0), axis=-1, keepdims=True)
        s_hi = jnp.sum(jnp.where(lo, 0.0, sq), axis=-1, keepdims=True)
        inv = jnp.where(lo, lax.rsqrt(s_lo / HEAD_DIM_A + EPS), lax.rsqrt(s_hi / HEAD_DIM_A + EPS))
        return c * inv * gain

    for j in range(WIDTH_A // LANES):
        c = pa[:, OFF_QA + j * LANES: OFF_QA + (j + 1) * LANES]
        qa_ref[:, j * LANES:(j + 1) * LANES] = half_norm(c, qag_ref[...]).astype(BF16)
    ka_ref[...] = half_norm(pa[:, OFF_KA:OFF_VA], kag_ref[...]).astype(BF16)
    va_ref[...] = pa[:, OFF_VA:OFF_CQ].astype(BF16)

    pm = _dot_nt(wm_ref[...], h)
    cq = _rms_rows(pm[0:Q_LORA_RANK], cqn_ref[...], Q_LORA_RANK).astype(BF16)
    ckv = _rms_rows(pm[Q_LORA_RANK:Q_LORA_RANK + KV_LORA_RANK], ckvn_ref[...], KV_LORA_RANK).astype(BF16)
    kr = pm[Q_LORA_RANK + KV_LORA_RANK:MLA_IN]
    qup = _dot(wuq_ref[...], cq)
    kvup = _dot(wukv_ref[...], ckv)
    cos = cos_ref[...]
    sin = sin_ref[...]

    def norm_rope(c, gain):
        y = _rms_rows(c, gain, QK_DIM_B)
        y1 = y[QK_NOPE_DIM:QK_NOPE_DIM + ROPE_HALF]
        y2 = y[QK_NOPE_DIM + ROPE_HALF:QK_DIM_B]
        return jnp.concatenate([y[0:QK_NOPE_DIM], y1 * cos - y2 * sin, y2 * cos + y1 * sin],
                               axis=0).astype(BF16)

    per_head = QK_NOPE_DIM + V_DIM_B
    for hd in range(N_HEADS_B):
        qb_ref[hd] = norm_rope(qup[hd * QK_DIM_B:(hd + 1) * QK_DIM_B], qbg_ref[...])
        k_nope = kvup[hd * per_head:hd * per_head + QK_NOPE_DIM]
        kb_ref[hd] = norm_rope(jnp.concatenate([k_nope, kr], axis=0), kbg_ref[...])
        v = kvup[hd * per_head + QK_NOPE_DIM:(hd + 1) * per_head]
        vb_ref[hd // 2, (hd % 2) * V_DIM_B:(hd % 2 + 1) * V_DIM_B, :] = v.astype(BF16)


def _mix_proj(x, g, wa, wm, cqn, wuq, ckvn, wukv, qag, kag, qbg, kbg, cos_t, sin_t):
    t = x.shape[0]
    tm = TOKEN_TILE

    def row(w):
        return pl.BlockSpec((tm, w), lambda i: (i, 0))

    def cols(*lead):
        return pl.BlockSpec((*lead, tm), lambda i: (*(0,) * len(lead), i))

    qk_shape = jax.ShapeDtypeStruct((N_HEADS_B, QK_DIM_B, t), BF16)
    return pl.pallas_call(
        _mix_proj_kernel,
        grid=(t // tm,),
        in_specs=[row(D_MODEL), _const_spec((1, D_MODEL)),
                  _const_spec((D_MODEL, SWA_IN)), _const_spec((MLA_IN, D_MODEL)),
                  _const_spec((Q_LORA_RANK, 1)), _const_spec((N_HEADS_B * QK_DIM_B, Q_LORA_RANK)),
                  _const_spec((KV_LORA_RANK, 1)),
                  _const_spec((N_HEADS_B * (QK_NOPE_DIM + V_DIM_B), KV_LORA_RANK)),
                  _const_spec((1, LANES)), _const_spec((1, LANES)),
                  _const_spec((QK_DIM_B, 1)), _const_spec((QK_DIM_B, 1)),
                  cols(ROPE_HALF), cols(ROPE_HALF)],
        out_specs=[row(WIDTH_A), row(LANES), row(LANES),
                   cols(N_HEADS_B, QK_DIM_B), cols(N_HEADS_B, QK_DIM_B), cols(N_HEADS_B // 2, LANES)],
        out_shape=[jax.ShapeDtypeStruct((t, w), BF16) for w in (WIDTH_A, LANES, LANES)]
        + [qk_shape, qk_shape, jax.ShapeDtypeStruct((N_HEADS_B // 2, LANES, t), BF16)],
        compiler_params=_params(1),
        name="mix_proj",
    )(x, g, wa, wm, cqn, wuq, ckvn, wukv, qag, kag, qbg, kbg, cos_t, sin_t)


SWA_BLOCKS_PER_STEP = 8


def _swa_blocks(step, sink_ref, q_ref, k_ref, v_ref, pq_ref, pk_ref, o_ref):
    blk = WINDOW
    lane = lax.broadcasted_iota(jnp.int32, (1, LANES), 1)
    lo = lane < HALF
    qi = lax.broadcasted_iota(jnp.int32, (blk, 2 * blk), 0) + blk
    ki = lax.broadcasted_iota(jnp.int32, (blk, 2 * blk), 1)
    diff = qi - ki
    band = (diff >= 0) & (diff < WINDOW)
    zero = jnp.zeros((), BF16)

    def block(r):
        cur = step * SWA_BLOCKS_PER_STEP + r
        rows = slice(r * blk, (r + 1) * blk)
        if r == 0:
            prev = jnp.maximum(cur - 1, 0)
            p0 = pl.multiple_of(prev * blk, blk)
            c0 = pl.multiple_of(cur * blk, blk)
            kk = jnp.concatenate([k_ref[0, pl.ds(p0, blk), :], k_ref[0, pl.ds(c0, blk), :]], axis=0)
            vv = jnp.concatenate([v_ref[0, pl.ds(p0, blk), :], v_ref[0, pl.ds(c0, blk), :]], axis=0)
            valid = band & ((ki >= blk) | (step > 0))
        else:
            prev = cur - 1
            p0 = pl.multiple_of(prev * blk, blk)
            kk = k_ref[0, pl.ds(p0, 2 * blk), :]
            vv = v_ref[0, pl.ds(p0, 2 * blk), :]
            valid = band
        pk = jnp.concatenate([pk_ref[0, prev], pk_ref[0, cur]], axis=1)
        dist = jnp.where(valid, (pq_ref[0, rows, :] - pk).astype(F32), MASK_DIST)

        o_kv, sink_kv = [], []
        for kv in range(N_KV_HEADS_A):
            keep = lo if kv == 0 else jnp.logical_not(lo)
            qs = [jnp.where(keep, q_ref[0, rows, j * LANES:(j + 1) * LANES], zero) for j in range(GROUP_A)]
            s_all = _dot_nt(jnp.concatenate(qs, axis=0), kk)
            ps, sink_terms = [], []
            for j in range(GROUP_A):
                hd = kv * GROUP_A + j
                slope = 2.0 ** (-8.0 * (hd + 1) / N_HEADS_A)
                sink = sink_ref[hd]
                s = s_all[j * blk:(j + 1) * blk] - slope * dist
                m = jnp.maximum(jnp.max(s, axis=-1, keepdims=True), sink)
                ps.append(jnp.exp(s - m).astype(BF16))
                sink_terms.append(jnp.exp(sink - m))
            v_ext = jnp.where(keep, vv, jnp.ones((), BF16))
            o_kv.append(_dot(jnp.concatenate(ps, axis=0), v_ext))
            sink_kv.append(sink_terms)
        for j in range(GROUP_A):
            o0 = o_kv[0][j * blk:(j + 1) * blk]
            o1 = o_kv[1][j * blk:(j + 1) * blk]
            num = jnp.where(lo, o0, o1)
            den = pltpu.roll(jnp.where(lo, o1, o0), HALF, 1) + jnp.where(lo, sink_kv[0][j], sink_kv[1][j])
            o_ref[0, rows, j * LANES:(j + 1) * LANES] = (num / den).astype(BF16)

    return [functools.partial(block, r) for r in range(SWA_BLOCKS_PER_STEP)]


def _mla_units(q_ref, k_ref, v_ref, o_ref):
    s_len = q_ref.shape[2]
    tq = ATTN_BLOCK_B
    c = QK_DIM_B ** -0.5 * np.log2(np.e)
    causal = (lax.broadcasted_iota(jnp.int32, (tq, tq), 0)
              >= lax.broadcasted_iota(jnp.int32, (tq, tq), 1))
    ones = jnp.ones((V_DIM_B, s_len), BF16)
    lo = lax.broadcasted_iota(jnp.int32, (1, LANES), 1) < HALF

    v_exts = []
    for pair in range(MLA_PAIRS_PER_STEP):
        v = v_ref[pair]
        v_exts.append((jnp.concatenate([v[0:V_DIM_B], ones], axis=0),
                       jnp.concatenate([ones, v[V_DIM_B:]], axis=0)))

    def unit(pair, i):
        n = (i + 1) * tq
        o = []
        for half in range(2):
            hd = 2 * pair + half
            q = q_ref[hd, :, i * tq:n].T
            s = _dot(q, k_ref[hd, :, 0:n])
            diag = jnp.where(causal, s[:, n - tq:], NEG)
            s = diag if i == 0 else jnp.concatenate([s[:, :n - tq], diag], axis=1)
            m = jnp.max(s, axis=-1, keepdims=True)
            p = jnp.exp2((s - m) * c).astype(BF16)
            o.append(_dot_nt(p, v_exts[pair][half][:, 0:n]))
        num = jnp.where(lo, o[0], o[1])
        den = pltpu.roll(jnp.where(lo, o[1], o[0]), HALF, 1)
        o_ref[0, i * tq:n, pair * LANES:(pair + 1) * LANES] = (num / den).astype(BF16)

    return [functools.partial(unit, pair, i)
            for i in reversed(range(s_len // tq)) for pair in range(MLA_PAIRS_PER_STEP)]


def _attn_kernel(sink_ref, qa_ref, ka_ref, va_ref, pq_ref, pk_ref, qb_ref, kb_ref, vb_ref, oa_ref, ob_ref):
    swa = _swa_blocks(pl.program_id(1), sink_ref, qa_ref, ka_ref, va_ref, pq_ref, pk_ref, oa_ref)
    mla = _mla_units(qb_ref, kb_ref, vb_ref, ob_ref)
    for k in range(max(len(swa), len(mla))):
        for items in (mla, swa):
            if k < len(items):
                items[k]()


def _attention(sinks, qa, ka, va, pos_col, pos_blocks, qb, kb, vb):
    b, s, _ = qa.shape
    rows = SWA_BLOCKS_PER_STEP * WINDOW
    pp = MLA_PAIRS_PER_STEP
    assert s // rows == N_HEADS_B // (2 * pp)
    cur = lambda bi, j: (bi, j, 0)
    whole = lambda bi, j: (bi, 0, 0)
    qk_spec = pl.BlockSpec((2 * pp, QK_DIM_B, s), lambda bi, j: (j, 0, bi))
    out = jax.ShapeDtypeStruct((b, s, WIDTH_A), BF16)
    return pl.pallas_call(
        _attn_kernel,
        grid=(b, s // rows),
        in_specs=[pl.BlockSpec(memory_space=pltpu.SMEM),
                  pl.BlockSpec((1, rows, WIDTH_A), cur),
                  pl.BlockSpec((1, s, LANES), whole), pl.BlockSpec((1, s, LANES), whole),
                  pl.BlockSpec((1, rows, 1), cur),
                  pl.BlockSpec((1, s // WINDOW, 1, WINDOW), lambda bi, j: (bi, 0, 0, 0)),
                  qk_spec, qk_spec, pl.BlockSpec((pp, LANES, s), lambda bi, j: (j, 0, bi))],
        out_specs=[pl.BlockSpec((1, rows, WIDTH_A), cur),
                   pl.BlockSpec((1, s, pp * LANES), lambda bi, j: (bi, 0, j))],
        out_shape=[out, out],
        compiler_params=_params(2),
        name="attention",
    )(sinks, qa, ka, va, pos_col, pos_blocks, qb, kb, vb)


def _merge_kernel(x_ref, g_ref, oa_ref, ob_ref, wg_ref, wba_ref, wbb_ref, wo_ref, o_ref):
    x = x_ref[...]
    h = _rms(x, g_ref[...], D_MODEL).astype(BF16)
    gates = _dot(h, wg_ref[...])
    ma = _dot(oa_ref[...], wba_ref[...])
    mb = _dot(ob_ref[...], wbb_ref[...])
    merged = jax.nn.sigmoid(gates[:, :D_MODEL]) * ma + jax.nn.sigmoid(gates[:, D_MODEL:]) * mb
    o_ref[...] = x + _dot(merged.astype(BF16), wo_ref[...])


def _merge(x, g, oa, ob, wg, wba, wbb, wo):
    t = x.shape[0]
    tm = TOKEN_TILE

    def row(w):
        return pl.BlockSpec((tm, w), lambda i: (i, 0))

    return pl.pallas_call(
        _merge_kernel,
        grid=(t // tm,),
        in_specs=[row(D_MODEL), _const_spec((1, D_MODEL)), row(WIDTH_A), row(WIDTH_B),
                  _const_spec((D_MODEL, 2 * D_MODEL)), _const_spec((WIDTH_A, D_MODEL)),
                  _const_spec((WIDTH_B, D_MODEL)), _const_spec((D_MODEL, D_MODEL))],
        out_specs=row(D_MODEL),
        out_shape=jax.ShapeDtypeStruct((t, D_MODEL), F32),
        compiler_params=_params(1),
        name="merge",
    )(x, g, oa, ob, wg, wba, wbb, wo)


def kernel(x, positions, ffn1_norm, ffn1_w_gate, ffn1_w_up, ffn1_w_down, mix_norm, w_in, swa_q_norm, swa_k_norm, swa_sinks, mla_q_lora_norm, mla_w_uq, mla_kv_lora_norm, mla_w_ukv, mla_q_norm, mla_k_norm, w_branch_a, w_branch_b, w_out, ffn2_norm, ffn2_w_gate, ffn2_w_up, ffn2_w_down):
    b, s, d = x.shape
    t = b * s
    depth = w_in.shape[0]
    assert d == D_MODEL and s % ATTN_BLOCK_B == 0 and t % TOKEN_TILE == 0 and t % min(ROPE_TILE, t) == 0
    assert s % (SWA_BLOCKS_PER_STEP * WINDOW) == 0

    cos_t, sin_t = _rope_tables(positions.reshape(1, t))
    pos_col3 = positions.reshape(b, s, 1)
    pos_blocks = positions.reshape(b, s // WINDOW, 1, WINDOW)

    ffn1_stack = (ffn1_w_gate, ffn1_w_up, ffn1_w_down)
    ffn2_stack = (ffn2_w_gate, ffn2_w_up, ffn2_w_down)
    ffn_w = tuple(w[0].astype(BF16) for w in ffn1_stack)

    xt = x.reshape(t, d)
    for l in range(depth):
        xt, ffn_w = _ffn(xt, ffn1_norm[l].reshape(1, d), *ffn_w, next_weights=ffn2_stack, next_layer=l)

        wi = w_in[l]
        wa = jnp.concatenate([jnp.take(wi, OFF_QA + _SWA_COLS, axis=1), wi[:, OFF_KA:OFF_CQ]],
                             axis=1).astype(BF16)
        wm = wi[:, OFF_CQ:OFF_GA].T.astype(BF16)
        qag = (jnp.tile(swa_q_norm[l], 2) * HEAD_DIM_A ** -0.5).reshape(1, LANES)
        kag = jnp.tile(swa_k_norm[l], 2).reshape(1, LANES)
        qa, ka, va, qb, kb, vb = _mix_proj(
            xt, mix_norm[l].reshape(1, d), wa, wm,
            mla_q_lora_norm[l].reshape(-1, 1), mla_w_uq[l].T.astype(BF16),
            mla_kv_lora_norm[l].reshape(-1, 1), mla_w_ukv[l].T.astype(BF16), qag, kag,
            mla_q_norm[l].reshape(-1, 1), mla_k_norm[l].reshape(-1, 1), cos_t, sin_t)

        oa, ob = _attention(swa_sinks[l], qa.reshape(b, s, -1), ka.reshape(b, s, -1), va.reshape(b, s, -1),
                            pos_col3, pos_blocks, qb, kb, vb)

        xt = _merge(xt, mix_norm[l].reshape(1, d), oa.reshape(t, -1), ob.reshape(t, -1),
                    wi[:, OFF_GA:].astype(BF16), jnp.take(w_branch_a[l], _SWA_COLS, axis=0).astype(BF16),
                    w_branch_b[l].astype(BF16), w_out[l].astype(BF16))

        last = l + 1 == depth
        xt, ffn_w = _ffn(xt, ffn2_norm[l].reshape(1, d), *ffn_w,
                         next_weights=() if last else ffn1_stack, next_layer=l + 1)
    return xt.reshape(b, s, d)
```

```python
import functools
from typing import Callable, NamedTuple

import numpy as np
import jax
import jax.numpy as jnp
from jax import lax
from jax.experimental import pallas as pl
from jax.experimental.pallas import tpu as pltpu

F32 = jnp.float32
BF16 = jnp.bfloat16

D_MODEL = 1024
D_FF = 2816
HEAD_DIM_A = 64
N_HEADS_A = 8
N_KV_HEADS_A = 2
GROUP_A = N_HEADS_A // N_KV_HEADS_A
WINDOW = 128
N_HEADS_B = 8
Q_LORA_RANK = 256
KV_LORA_RANK = 128
QK_NOPE_DIM = 64
QK_ROPE_DIM = 32
QK_DIM_B = QK_NOPE_DIM + QK_ROPE_DIM
V_DIM_B = 64
ROPE_BASE = 10000.0
WIDTH_A = N_HEADS_A * HEAD_DIM_A
WIDTH_B = N_HEADS_B * V_DIM_B
EPS = 1e-6
NEG = -1e30
MASK_DIST = 1e30

LANES = 128
HALF = LANES // 2
BF16_SUBLANES = 16

OFF_QA = 0
OFF_KA = OFF_QA + WIDTH_A
OFF_VA = OFF_KA + N_KV_HEADS_A * HEAD_DIM_A
OFF_CQ = OFF_VA + N_KV_HEADS_A * HEAD_DIM_A
OFF_CKV = OFF_CQ + Q_LORA_RANK
OFF_KR = OFF_CKV + KV_LORA_RANK
OFF_GA = OFF_KR + QK_ROPE_DIM
OFF_GB = OFF_GA + D_MODEL
IN_WIDTH = OFF_GB + D_MODEL

TOKEN_TILE = 1024
FF_CHUNK = 256
ATTN_BLOCK_B = 512
MLA_PAIRS_PER_STEP = 2
ROPE_TILE = 4096
VMEM_LIMIT = 56 * 1024 * 1024

ROPE_HALF = QK_ROPE_DIM // 2
SWA_IN = OFF_CQ
MLA_IN = Q_LORA_RANK + KV_LORA_RANK + QK_ROPE_DIM

_SWA_HEAD_ORDER = np.array([h for j in range(GROUP_A) for h in (j, j + GROUP_A)])


def _rms(x, g, n):
    ssq = jnp.sum(x * x, axis=-1, keepdims=True)
    return x * lax.rsqrt(ssq / n + EPS) * g


def _rms_rows(c, g, n):
    ssq = jnp.sum(c * c, axis=0, keepdims=True)
    return c * lax.rsqrt(ssq / n + EPS) * g


def _dot(a, b):
    return jnp.dot(a, b, preferred_element_type=F32)


def _dot_nt(a, b):
    return lax.dot_general(a, b, (((1,), (1,)), ((), ())), preferred_element_type=F32)


def _const_spec(shape):
    return pl.BlockSpec(shape, lambda *_: (0,) * len(shape), pipeline_mode=pl.Buffered(1))


def _params(n_axes):
    return pltpu.CompilerParams(dimension_semantics=("parallel",) * n_axes,
                                vmem_limit_bytes=VMEM_LIMIT)


class _CastJob(NamedTuple):
    src: jax.Array
    in_spec: pl.BlockSpec
    out_spec: pl.BlockSpec
    out_shape: tuple
    select: Callable = lambda v: v


def _cast_job(src, layer, n_steps, select=lambda v: v, out_cols=None, blk=None, row_block=lambda i: i):
    _, rows, cols = src.shape
    if blk is None:
        blk = next(r for r in range(BF16_SUBLANES, rows + 1, BF16_SUBLANES)
                   if rows % r == 0 and rows // r <= n_steps)
    assert rows % blk == 0 and rows // blk <= n_steps
    last = rows // blk - 1
    out_cols = cols if out_cols is None else out_cols
    return _CastJob(src,
                    pl.BlockSpec((None, blk, cols), lambda i: (layer, row_block(jnp.minimum(i, last)), 0)),
                    pl.BlockSpec((blk, out_cols), lambda i: (jnp.minimum(i, last), 0)),
                    (rows, out_cols), select)


def _ffn_kernel(*refs, selects):
    n_jobs = len(selects)
    x_ref, g_ref, wg_ref, wu_ref, wd_ref = refs[:5]
    o_ref = refs[5 + n_jobs]
    x = x_ref[...]
    h = _rms(x, g_ref[...], D_MODEL).astype(BF16)
    y = None
    for c in range(D_FF // FF_CHUNK):
        sl = slice(c * FF_CHUNK, (c + 1) * FF_CHUNK)
        gate = _dot(h, wg_ref[:, sl])
        up = _dot(h, wu_ref[:, sl])
        act = (gate * jax.nn.sigmoid(gate) * up).astype(BF16)
        d = _dot(act, wd_ref[sl, :])
        y = d if y is None else y + d
    o_ref[...] = x + 0.5 * y
    for select, src_ref, dst_ref in zip(selects, refs[5:5 + n_jobs], refs[6 + n_jobs:]):
        dst_ref[...] = select(src_ref[...]).astype(BF16)


def _ffn(x, g, wg, wu, wd, make_jobs=lambda n_steps: ()):
    t = x.shape[0]
    tm = TOKEN_TILE
    n_steps = t // tm
    jobs = tuple(make_jobs(n_steps))
    row = pl.BlockSpec((tm, D_MODEL), lambda i: (i, 0))
    out = pl.pallas_call(
        functools.partial(_ffn_kernel, selects=tuple(j.select for j in jobs)),
        grid=(n_steps,),
        in_specs=[row, _const_spec((1, D_MODEL)), _const_spec((D_MODEL, D_FF)),
                  _const_spec((D_MODEL, D_FF)), _const_spec((D_FF, D_MODEL))] + [j.in_spec for j in jobs],
        out_specs=[row] + [j.out_spec for j in jobs],
        out_shape=[jax.ShapeDtypeStruct((t, D_MODEL), F32)]
        + [jax.ShapeDtypeStruct(j.out_shape, BF16) for j in jobs],
        compiler_params=pltpu.CompilerParams(dimension_semantics=("arbitrary",),
                                             vmem_limit_bytes=VMEM_LIMIT),
        name="ffn",
    )(x, g, wg, wu, wd, *[j.src for j in jobs])
    return out[0], tuple(out[1:])


def _rope_kernel(pos_ref, invf_ref, c_ref, s_ref):
    ang = pos_ref[...].astype(F32) * invf_ref[...]
    c_ref[...] = jnp.cos(ang)
    s_ref[...] = jnp.sin(ang)


def _rope_tables(pos_row):
    t = pos_row.shape[1]
    tm = min(ROPE_TILE, t)
    inv_freq = ROPE_BASE ** (-jnp.arange(ROPE_HALF, dtype=F32) / ROPE_HALF)
    out = jax.ShapeDtypeStruct((ROPE_HALF, t), F32)
    return pl.pallas_call(
        _rope_kernel,
        grid=(t // tm,),
        in_specs=[pl.BlockSpec((1, tm), lambda i: (0, i)), _const_spec((ROPE_HALF, 1))],
        out_specs=[pl.BlockSpec((ROPE_HALF, tm), lambda i: (0, i))] * 2,
        out_shape=[out, out],
        compiler_params=_params(1),
        name="rope_tables",
    )(pos_row, inv_freq.reshape(ROPE_HALF, 1))


def _mix_proj_kernel(x_ref, g_ref, wa_ref, wm_ref, cqn_ref, wuq_ref, ckvn_ref, wukv_ref,
                     qag_ref, kag_ref, qbg_ref, kbg_ref, cos_ref, sin_ref,
                     qa_ref, ka_ref, va_ref, qb_ref, kb_ref, vb_ref):
    x = x_ref[...]
    h = _rms(x, g_ref[...], D_MODEL).astype(BF16)

    pa = _dot(h, wa_ref[...])
    lane = lax.broadcasted_iota(jnp.int32, (1, LANES), 1)
    lo = lane < HALF

    def half_norm(c, gain):
        sq = c * c
        s_lo = jnp.sum(jnp.where(lo, sq, 0.0), axis=-1, keepdims=True)
        s_hi = jnp.sum(jnp.where(lo, 0.0, sq), axis=-1, keepdims=True)
        inv = jnp.where(lo, lax.rsqrt(s_lo / HEAD_DIM_A + EPS), lax.rsqrt(s_hi / HEAD_DIM_A + EPS))
        return c * inv * gain

    for j in range(WIDTH_A // LANES):
        c = pa[:, OFF_QA + j * LANES: OFF_QA + (j + 1) * LANES]
        qa_ref[:, j * LANES:(j + 1) * LANES] = half_norm(c, qag_ref[...]).astype(BF16)
    ka_ref[...] = half_norm(pa[:, OFF_KA:OFF_VA], kag_ref[...]).astype(BF16)
    va_ref[...] = pa[:, OFF_VA:OFF_CQ].astype(BF16)

    pm = _dot_nt(wm_ref[...], h)
    cq = _rms_rows(pm[0:Q_LORA_RANK], cqn_ref[...], Q_LORA_RANK).astype(BF16)
    ckv = _rms_rows(pm[Q_LORA_RANK:Q_LORA_RANK + KV_LORA_RANK], ckvn_ref[...], KV_LORA_RANK).astype(BF16)
    kr = pm[Q_LORA_RANK + KV_LORA_RANK:MLA_IN]
    qup = _dot(wuq_ref[...], cq)
    kvup = _dot(wukv_ref[...], ckv)
    cos = cos_ref[...]
    sin = sin_ref[...]

    def norm_rope(c, gain):
        y = _rms_rows(c, gain, QK_DIM_B)
        y1 = y[QK_NOPE_DIM:QK_NOPE_DIM + ROPE_HALF]
        y2 = y[QK_NOPE_DIM + ROPE_HALF:QK_DIM_B]
        return jnp.concatenate([y[0:QK_NOPE_DIM], y1 * cos - y2 * sin, y2 * cos + y1 * sin],
                               axis=0).astype(BF16)

    per_head = QK_NOPE_DIM + V_DIM_B
    for hd in range(N_HEADS_B):
        qb_ref[hd] = norm_rope(qup[hd * QK_DIM_B:(hd + 1) * QK_DIM_B], qbg_ref[...])
        k_nope = kvup[hd * per_head:hd * per_head + QK_NOPE_DIM]
        kb_ref[hd] = norm_rope(jnp.concatenate([k_nope, kr], axis=0), kbg_ref[...])
        v = kvup[hd * per_head + QK_NOPE_DIM:(hd + 1) * per_head]
        vb_ref[hd // 2, (hd % 2) * V_DIM_B:(hd % 2 + 1) * V_DIM_B, :] = v.astype(BF16)


def _mix_proj(x, g, wa, wm, cqn, wuq, ckvn, wukv, qag, kag, qbg, kbg, cos_t, sin_t):
    t = x.shape[0]
    tm = TOKEN_TILE

    def row(w):
        return pl.BlockSpec((tm, w), lambda i: (i, 0))

    def cols(*lead):
        return pl.BlockSpec((*lead, tm), lambda i: (*(0,) * len(lead), i))

    qk_shape = jax.ShapeDtypeStruct((N_HEADS_B, QK_DIM_B, t), BF16)
    return pl.pallas_call(
        _mix_proj_kernel,
        grid=(t // tm,),
        in_specs=[row(D_MODEL), _const_spec((1, D_MODEL)),
                  _const_spec((D_MODEL, SWA_IN)), _const_spec((MLA_IN, D_MODEL)),
                  _const_spec((Q_LORA_RANK, 1)), _const_spec((N_HEADS_B * QK_DIM_B, Q_LORA_RANK)),
                  _const_spec((KV_LORA_RANK, 1)),
                  _const_spec((N_HEADS_B * (QK_NOPE_DIM + V_DIM_B), KV_LORA_RANK)),
                  _const_spec((1, LANES)), _const_spec((1, LANES)),
                  _const_spec((QK_DIM_B, 1)), _const_spec((QK_DIM_B, 1)),
                  cols(ROPE_HALF), cols(ROPE_HALF)],
        out_specs=[row(WIDTH_A), row(LANES), row(LANES),
                   cols(N_HEADS_B, QK_DIM_B), cols(N_HEADS_B, QK_DIM_B), cols(N_HEADS_B // 2, LANES)],
        out_shape=[jax.ShapeDtypeStruct((t, w), BF16) for w in (WIDTH_A, LANES, LANES)]
        + [qk_shape, qk_shape, jax.ShapeDtypeStruct((N_HEADS_B // 2, LANES, t), BF16)],
        compiler_params=_params(1),
        name="mix_proj",
    )(x, g, wa, wm, cqn, wuq, ckvn, wukv, qag, kag, qbg, kbg, cos_t, sin_t)


SWA_BLOCKS_PER_STEP = 8


def _swa_blocks(step, sink_ref, q_ref, k_ref, v_ref, pq_ref, pk_ref, o_ref):
    blk = WINDOW
    lane = lax.broadcasted_iota(jnp.int32, (1, LANES), 1)
    lo = lane < HALF
    qi = lax.broadcasted_iota(jnp.int32, (blk, 2 * blk), 0) + blk
    ki = lax.broadcasted_iota(jnp.int32, (blk, 2 * blk), 1)
    diff = qi - ki
    band = (diff >= 0) & (diff < WINDOW)
    zero = jnp.zeros((), BF16)

    def block(r):
        cur = step * SWA_BLOCKS_PER_STEP + r
        rows = slice(r * blk, (r + 1) * blk)
        if r == 0:
            prev = jnp.maximum(cur - 1, 0)
            p0 = pl.multiple_of(prev * blk, blk)
            c0 = pl.multiple_of(cur * blk, blk)
            kk = jnp.concatenate([k_ref[0, pl.ds(p0, blk), :], k_ref[0, pl.ds(c0, blk), :]], axis=0)
            vv = jnp.concatenate([v_ref[0, pl.ds(p0, blk), :], v_ref[0, pl.ds(c0, blk), :]], axis=0)
            valid = band & ((ki >= blk) | (step > 0))
        else:
            prev = cur - 1
            p0 = pl.multiple_of(prev * blk, blk)
            kk = k_ref[0, pl.ds(p0, 2 * blk), :]
            vv = v_ref[0, pl.ds(p0, 2 * blk), :]
            valid = band
        pk = jnp.concatenate([pk_ref[0, prev], pk_ref[0, cur]], axis=1)
        dist = jnp.where(valid, (pq_ref[0, rows, :] - pk).astype(F32), MASK_DIST)

        o_kv, sink_kv = [], []
        for kv in range(N_KV_HEADS_A):
            keep = lo if kv == 0 else jnp.logical_not(lo)
            qs = [jnp.where(keep, q_ref[0, rows, j * LANES:(j + 1) * LANES], zero) for j in range(GROUP_A)]
            s_all = _dot_nt(jnp.concatenate(qs, axis=0), kk)
            ps, sink_terms = [], []
            for j in range(GROUP_A):
                hd = kv * GROUP_A + j
                slope = 2.0 ** (-8.0 * (hd + 1) / N_HEADS_A)
                sink = sink_ref[hd]
                s = s_all[j * blk:(j + 1) * blk] - slope * dist
                m = jnp.maximum(jnp.max(s, axis=-1, keepdims=True), sink)
                ps.append(jnp.exp(s - m).astype(BF16))
                sink_terms.append(jnp.exp(sink - m))
            v_ext = jnp.where(keep, vv, jnp.ones((), BF16))
            o_kv.append(_dot(jnp.concatenate(ps, axis=0), v_ext))
            sink_kv.append(sink_terms)
        for j in range(GROUP_A):
            o0 = o_kv[0][j * blk:(j + 1) * blk]
            o1 = o_kv[1][j * blk:(j + 1) * blk]
            num = jnp.where(lo, o0, o1)
            den = pltpu.roll(jnp.where(lo, o1, o0), HALF, 1) + jnp.where(lo, sink_kv[0][j], sink_kv[1][j])
            o_ref[0, rows, j * LANES:(j + 1) * LANES] = (num / den).astype(BF16)

    return [functools.partial(block, r) for r in range(SWA_BLOCKS_PER_STEP)]


def _mla_units(q_ref, k_ref, v_ref, o_ref):
    s_len = q_ref.shape[2]
    tq = ATTN_BLOCK_B
    c = QK_DIM_B ** -0.5 * np.log2(np.e)
    causal = (lax.broadcasted_iota(jnp.int32, (tq, tq), 0)
              >= lax.broadcasted_iota(jnp.int32, (tq, tq), 1))
    ones = jnp.ones((V_DIM_B, s_len), BF16)
    lo = lax.broadcasted_iota(jnp.int32, (1, LANES), 1) < HALF

    v_exts = []
    for pair in range(MLA_PAIRS_PER_STEP):
        v = v_ref[pair]
        v_exts.append((jnp.concatenate([v[0:V_DIM_B], ones], axis=0),
                       jnp.concatenate([ones, v[V_DIM_B:]], axis=0)))

    def unit(pair, i):
        n = (i + 1) * tq
        o = []
        for half in range(2):
            hd = 2 * pair + half
            q = q_ref[hd, :, i * tq:n].T
            s = _dot(q, k_ref[hd, :, 0:n])
            diag = jnp.where(causal, s[:, n - tq:], NEG)
            s = diag if i == 0 else jnp.concatenate([s[:, :n - tq], diag], axis=1)
            m = jnp.max(s, axis=-1, keepdims=True)
            p = jnp.exp2((s - m) * c).astype(BF16)
            o.append(_dot_nt(p, v_exts[pair][half][:, 0:n]))
        num = jnp.where(lo, o[0], o[1])
        den = pltpu.roll(jnp.where(lo, o[1], o[0]), HALF, 1)
        o_ref[0, i * tq:n, pair * LANES:(pair + 1) * LANES] = (num / den).astype(BF16)

    return [functools.partial(unit, pair, i)
            for i in reversed(range(s_len // tq)) for pair in range(MLA_PAIRS_PER_STEP)]


def _attn_kernel(sink_ref, qa_ref, ka_ref, va_ref, pq_ref, pk_ref, qb_ref, kb_ref, vb_ref, oa_ref, ob_ref):
    swa = _swa_blocks(pl.program_id(1), sink_ref, qa_ref, ka_ref, va_ref, pq_ref, pk_ref, oa_ref)
    mla = _mla_units(qb_ref, kb_ref, vb_ref, ob_ref)
    for k in range(max(len(swa), len(mla))):
        for items in (mla, swa):
            if k < len(items):
                items[k]()


def _attention(sinks, qa, ka, va, pos_col, pos_blocks, qb, kb, vb):
    b, s, _ = qa.shape
    rows = SWA_BLOCKS_PER_STEP * WINDOW
    pp = MLA_PAIRS_PER_STEP
    assert s // rows == N_HEADS_B // (2 * pp)
    cur = lambda bi, j: (bi, j, 0)
    whole = lambda bi, j: (bi, 0, 0)
    qk_spec = pl.BlockSpec((2 * pp, QK_DIM_B, s), lambda bi, j: (j, 0, bi))
    out = jax.ShapeDtypeStruct((b, s, WIDTH_A), BF16)
    return pl.pallas_call(
        _attn_kernel,
        grid=(b, s // rows),
        in_specs=[pl.BlockSpec(memory_space=pltpu.SMEM),
                  pl.BlockSpec((1, rows, WIDTH_A), cur),
                  pl.BlockSpec((1, s, LANES), whole), pl.BlockSpec((1, s, LANES), whole),
                  pl.BlockSpec((1, rows, 1), cur),
                  pl.BlockSpec((1, s // WINDOW, 1, WINDOW), lambda bi, j: (bi, 0, 0, 0)),
                  qk_spec, qk_spec, pl.BlockSpec((pp, LANES, s), lambda bi, j: (j, 0, bi))],
        out_specs=[pl.BlockSpec((1, rows, WIDTH_A), cur),
                   pl.BlockSpec((1, s, pp * LANES), lambda bi, j: (bi, 0, j))],
        out_shape=[out, out],
        compiler_params=_params(2),
        name="attention",
    )(sinks, qa, ka, va, pos_col, pos_blocks, qb, kb, vb)


def _merge_kernel(x_ref, g_ref, oa_ref, ob_ref, wg_ref, wba_ref, wbb_ref, wo_ref, o_ref):
    x = x_ref[...]
    h = _rms(x, g_ref[...], D_MODEL).astype(BF16)
    gates = _dot(h, wg_ref[...])
    ma = _dot(oa_ref[...], wba_ref[...])
    mb = _dot(ob_ref[...], wbb_ref[...])
    merged = jax.nn.sigmoid(gates[:, :D_MODEL]) * ma + jax.nn.sigmoid(gates[:, D_MODEL:]) * mb
    o_ref[...] = x + _dot(merged.astype(BF16), wo_ref[...])


def _merge(x, g, oa, ob, wg, wba, wbb, wo):
    t = x.shape[0]
    tm = TOKEN_TILE

    def row(w):
        return pl.BlockSpec((tm, w), lambda i: (i, 0))

    return pl.pallas_call(
        _merge_kernel,
        grid=(t // tm,),
        in_specs=[row(D_MODEL), _const_spec((1, D_MODEL)), row(WIDTH_A), row(WIDTH_B),
                  _const_spec((D_MODEL, 2 * D_MODEL)), _const_spec((WIDTH_A, D_MODEL)),
                  _const_spec((WIDTH_B, D_MODEL)), _const_spec((D_MODEL, D_MODEL))],
        out_specs=row(D_MODEL),
        out_shape=jax.ShapeDtypeStruct((t, D_MODEL), F32),
        compiler_params=_params(1),
        name="merge",
    )(x, g, oa, ob, wg, wba, wbb, wo)


def kernel(x, positions, ffn1_norm, ffn1_w_gate, ffn1_w_up, ffn1_w_down, mix_norm, w_in, swa_q_norm, swa_k_norm, swa_sinks, mla_q_lora_norm, mla_w_uq, mla_kv_lora_norm, mla_w_ukv, mla_q_norm, mla_k_norm, w_branch_a, w_branch_b, w_out, ffn2_norm, ffn2_w_gate, ffn2_w_up, ffn2_w_down):
    b, s, d = x.shape
    t = b * s
    depth = w_in.shape[0]
    assert d == D_MODEL and s % ATTN_BLOCK_B == 0 and t % TOKEN_TILE == 0 and t % min(ROPE_TILE, t) == 0
    assert s % (SWA_BLOCKS_PER_STEP * WINDOW) == 0

    cos_t, sin_t = _rope_tables(positions.reshape(1, t))
    pos_col3 = positions.reshape(b, s, 1)
    pos_blocks = positions.reshape(b, s // WINDOW, 1, WINDOW)

    ffn1_stack = (ffn1_w_gate, ffn1_w_up, ffn1_w_down)
    ffn2_stack = (ffn2_w_gate, ffn2_w_up, ffn2_w_down)
    ffn_w = tuple(w[0].astype(BF16) for w in ffn1_stack)
    head_rows = WIDTH_A // len(_SWA_HEAD_ORDER)

    def swa_cols(v):
        heads = [v[:, OFF_QA + h * HEAD_DIM_A:OFF_QA + (h + 1) * HEAD_DIM_A] for h in _SWA_HEAD_ORDER]
        return jnp.concatenate(heads + [v[:, OFF_KA:OFF_CQ]], axis=1)

    xt = x.reshape(t, d)
    for l in range(depth):
        def mixer_jobs(n_steps, l=l):
            return ([_cast_job(w, l, n_steps) for w in ffn2_stack]
                    + [_cast_job(w_in, l, n_steps, swa_cols, SWA_IN),
                       _cast_job(w_in, l, n_steps, lambda v: v[:, OFF_GA:], 2 * D_MODEL),
                       _cast_job(w_branch_a, l, n_steps, blk=head_rows,
                                 row_block=lambda r: r // 2 + GROUP_A * (r % 2)),
                       _cast_job(w_branch_b, l, n_steps), _cast_job(w_out, l, n_steps)])

        xt, (*ffn_w, wa, w_gates, wba, wbb, wo) = _ffn(xt, ffn1_norm[l].reshape(1, d), *ffn_w,
                                                       make_jobs=mixer_jobs)

        wm = w_in[l][:, OFF_CQ:OFF_GA].T.astype(BF16)
        qag = (jnp.tile(swa_q_norm[l], 2) * HEAD_DIM_A ** -0.5).reshape(1, LANES)
        kag = jnp.tile(swa_k_norm[l], 2).reshape(1, LANES)
        qa, ka, va, qb, kb, vb = _mix_proj(
            xt, mix_norm[l].reshape(1, d), wa, wm,
            mla_q_lora_norm[l].reshape(-1, 1), mla_w_uq[l].T.astype(BF16),
            mla_kv_lora_norm[l].reshape(-1, 1), mla_w_ukv[l].T.astype(BF16), qag, kag,
            mla_q_norm[l].reshape(-1, 1), mla_k_norm[l].reshape(-1, 1), cos_t, sin_t)

        oa, ob = _attention(swa_sinks[l], qa.reshape(b, s, -1), ka.reshape(b, s, -1), va.reshape(b, s, -1),
                            pos_col3, pos_blocks, qb, kb, vb)

        xt = _merge(xt, mix_norm[l].reshape(1, d), oa.reshape(t, -1), ob.reshape(t, -1),
                    w_gates, wba, wbb, wo)

        def next_ffn_jobs(n_steps, l=l):
            return [_cast_job(w, l + 1, n_steps) for w in ffn1_stack] if l + 1 < depth else []

        xt, ffn_w = _ffn(xt, ffn2_norm[l].reshape(1, d), *ffn_w, make_jobs=next_ffn_jobs)
    return xt.reshape(b, s, d)
```

```python
import functools
from typing import Callable, NamedTuple

import numpy as np
import jax
import jax.numpy as jnp
from jax import lax
from jax.experimental import pallas as pl
from jax.experimental.pallas import tpu as pltpu

F32 = jnp.float32
BF16 = jnp.bfloat16

D_MODEL = 1024
D_FF = 2816
HEAD_DIM_A = 64
N_HEADS_A = 8
N_KV_HEADS_A = 2
GROUP_A = N_HEADS_A // N_KV_HEADS_A
WINDOW = 128
N_HEADS_B = 8
Q_LORA_RANK = 256
KV_LORA_RANK = 128
QK_NOPE_DIM = 64
QK_ROPE_DIM = 32
QK_DIM_B = QK_NOPE_DIM + QK_ROPE_DIM
V_DIM_B = 64
ROPE_BASE = 10000.0
WIDTH_A = N_HEADS_A * HEAD_DIM_A
WIDTH_B = N_HEADS_B * V_DIM_B
EPS = 1e-6
NEG = -1e30
MASK_DIST = 1e30

LANES = 128
HALF = LANES // 2
BF16_SUBLANES = 16

OFF_QA = 0
OFF_KA = OFF_QA + WIDTH_A
OFF_VA = OFF_KA + N_KV_HEADS_A * HEAD_DIM_A
OFF_CQ = OFF_VA + N_KV_HEADS_A * HEAD_DIM_A
OFF_CKV = OFF_CQ + Q_LORA_RANK
OFF_KR = OFF_CKV + KV_LORA_RANK
OFF_GA = OFF_KR + QK_ROPE_DIM
OFF_GB = OFF_GA + D_MODEL
IN_WIDTH = OFF_GB + D_MODEL

TOKEN_TILE = 1024
FF_CHUNK = 256
ATTN_BLOCK_B = 512
MLA_PAIRS_PER_STEP = 2
ROPE_TILE = 4096
VMEM_LIMIT = 56 * 1024 * 1024

ROPE_HALF = QK_ROPE_DIM // 2
SWA_IN = OFF_CQ
MLA_IN = Q_LORA_RANK + KV_LORA_RANK + QK_ROPE_DIM

_SWA_HEAD_ORDER = np.array([h for j in range(GROUP_A) for h in (j, j + GROUP_A)])


def _rms(x, g, n):
    ssq = jnp.sum(x * x, axis=-1, keepdims=True)
    return x * lax.rsqrt(ssq / n + EPS) * g


def _rms_rows(c, g, n):
    ssq = jnp.sum(c * c, axis=0, keepdims=True)
    return c * lax.rsqrt(ssq / n + EPS) * g


def _dot(a, b):
    return jnp.dot(a, b, preferred_element_type=F32)


def _dot_nt(a, b):
    return lax.dot_general(a, b, (((1,), (1,)), ((), ())), preferred_element_type=F32)


def _const_spec(shape):
    return pl.BlockSpec(shape, lambda *_: (0,) * len(shape), pipeline_mode=pl.Buffered(1))


def _params(n_axes):
    return pltpu.CompilerParams(dimension_semantics=("parallel",) * n_axes,
                                vmem_limit_bytes=VMEM_LIMIT)


class _CastJob(NamedTuple):
    src: jax.Array
    in_spec: pl.BlockSpec
    out_spec: pl.BlockSpec
    out_shape: tuple
    select: Callable = lambda v: v


def _cast_job(src, layer, n_steps, select=lambda v: v, out_cols=None, blk=None, row_block=lambda i: i,
              transposed=False):
    _, rows, cols = src.shape
    unit = LANES if transposed else BF16_SUBLANES
    if blk is None:
        blk = next(r for r in range(unit, rows + 1, unit) if rows % r == 0 and rows // r <= n_steps)
    assert rows % blk == 0 and rows // blk <= n_steps
    last = rows // blk - 1
    out_cols = cols if out_cols is None else out_cols
    in_spec = pl.BlockSpec((None, blk, cols), lambda i: (layer, row_block(jnp.minimum(i, last)), 0))
    if transposed:
        return _CastJob(src, in_spec, pl.BlockSpec((out_cols, blk), lambda i: (0, jnp.minimum(i, last))),
                        (out_cols, rows), select)
    return _CastJob(src, in_spec, pl.BlockSpec((blk, out_cols), lambda i: (jnp.minimum(i, last), 0)),
                    (rows, out_cols), select)


def _ffn_kernel(*refs, selects):
    n_jobs = len(selects)
    x_ref, g_ref, wg_ref, wu_ref, wd_ref = refs[:5]
    o_ref = refs[5 + n_jobs]
    x = x_ref[...]
    h = _rms(x, g_ref[...], D_MODEL).astype(BF16)
    y = None
    for c in range(D_FF // FF_CHUNK):
        sl = slice(c * FF_CHUNK, (c + 1) * FF_CHUNK)
        gate = _dot(h, wg_ref[:, sl])
        up = _dot(h, wu_ref[:, sl])
        act = (gate * jax.nn.sigmoid(gate) * up).astype(BF16)
        d = _dot(act, wd_ref[sl, :])
        y = d if y is None else y + d
    o_ref[...] = x + 0.5 * y
    for select, src_ref, dst_ref in zip(selects, refs[5:5 + n_jobs], refs[6 + n_jobs:]):
        dst_ref[...] = select(src_ref[...]).astype(BF16)


def _ffn(x, g, wg, wu, wd, make_jobs=lambda n_steps: ()):
    t = x.shape[0]
    tm = TOKEN_TILE
    n_steps = t // tm
    jobs = tuple(make_jobs(n_steps))
    row = pl.BlockSpec((tm, D_MODEL), lambda i: (i, 0))
    out = pl.pallas_call(
        functools.partial(_ffn_kernel, selects=tuple(j.select for j in jobs)),
        grid=(n_steps,),
        in_specs=[row, _const_spec((1, D_MODEL)), _const_spec((D_MODEL, D_FF)),
                  _const_spec((D_MODEL, D_FF)), _const_spec((D_FF, D_MODEL))] + [j.in_spec for j in jobs],
        out_specs=[row] + [j.out_spec for j in jobs],
        out_shape=[jax.ShapeDtypeStruct((t, D_MODEL), F32)]
        + [jax.ShapeDtypeStruct(j.out_shape, BF16) for j in jobs],
        compiler_params=pltpu.CompilerParams(dimension_semantics=("arbitrary",),
                                             vmem_limit_bytes=VMEM_LIMIT),
        name="ffn",
    )(x, g, wg, wu, wd, *[j.src for j in jobs])
    return out[0], tuple(out[1:])


def _rope_kernel(pos_ref, invf_ref, c_ref, s_ref):
    ang = pos_ref[...].astype(F32) * invf_ref[...]
    c_ref[...] = jnp.cos(ang)
    s_ref[...] = jnp.sin(ang)


def _rope_tables(pos_row):
    t = pos_row.shape[1]
    tm = min(ROPE_TILE, t)
    inv_freq = ROPE_BASE ** (-jnp.arange(ROPE_HALF, dtype=F32) / ROPE_HALF)
    out = jax.ShapeDtypeStruct((ROPE_HALF, t), F32)
    return pl.pallas_call(
        _rope_kernel,
        grid=(t // tm,),
        in_specs=[pl.BlockSpec((1, tm), lambda i: (0, i)), _const_spec((ROPE_HALF, 1))],
        out_specs=[pl.BlockSpec((ROPE_HALF, tm), lambda i: (0, i))] * 2,
        out_shape=[out, out],
        compiler_params=_params(1),
        name="rope_tables",
    )(pos_row, inv_freq.reshape(ROPE_HALF, 1))


def _mix_proj_kernel(x_ref, g_ref, wa_ref, wm_ref, cqn_ref, wuq_ref, ckvn_ref, wukv_ref,
                     qag_ref, kag_ref, qbg_ref, kbg_ref, cos_ref, sin_ref,
                     qa_ref, ka_ref, va_ref, qb_ref, kb_ref, vb_ref):
    x = x_ref[...]
    h = _rms(x, g_ref[...], D_MODEL).astype(BF16)

    pa = _dot(h, wa_ref[...])
    lane = lax.broadcasted_iota(jnp.int32, (1, LANES), 1)
    lo = lane < HALF

    def half_norm(c, gain):
        sq = c * c
        s_lo = jnp.sum(jnp.where(lo, sq, 0.0), axis=-1, keepdims=True)
        s_hi = jnp.sum(jnp.where(lo, 0.0, sq), axis=-1, keepdims=True)
        inv = jnp.where(lo, lax.rsqrt(s_lo / HEAD_DIM_A + EPS), lax.rsqrt(s_hi / HEAD_DIM_A + EPS))
        return c * inv * gain

    for j in range(WIDTH_A // LANES):
        c = pa[:, OFF_QA + j * LANES: OFF_QA + (j + 1) * LANES]
        qa_ref[:, j * LANES:(j + 1) * LANES] = half_norm(c, qag_ref[...]).astype(BF16)
    ka_ref[...] = half_norm(pa[:, OFF_KA:OFF_VA], kag_ref[...]).astype(BF16)
    va_ref[...] = pa[:, OFF_VA:OFF_CQ].astype(BF16)

    pm = _dot_nt(wm_ref[...], h)
    cq = _rms_rows(pm[0:Q_LORA_RANK], cqn_ref[...], Q_LORA_RANK).astype(BF16)
    ckv = _rms_rows(pm[Q_LORA_RANK:Q_LORA_RANK + KV_LORA_RANK], ckvn_ref[...], KV_LORA_RANK).astype(BF16)
    kr = pm[Q_LORA_RANK + KV_LORA_RANK:MLA_IN]
    qup = _dot(wuq_ref[...], cq)
    kvup = _dot(wukv_ref[...], ckv)
    cos = cos_ref[...]
    sin = sin_ref[...]

    def norm_rope(c, gain):
        y = _rms_rows(c, gain, QK_DIM_B)
        y1 = y[QK_NOPE_DIM:QK_NOPE_DIM + ROPE_HALF]
        y2 = y[QK_NOPE_DIM + ROPE_HALF:QK_DIM_B]
        return jnp.concatenate([y[0:QK_NOPE_DIM], y1 * cos - y2 * sin, y2 * cos + y1 * sin],
                               axis=0).astype(BF16)

    per_head = QK_NOPE_DIM + V_DIM_B
    for hd in range(N_HEADS_B):
        qb_ref[hd] = norm_rope(qup[hd * QK_DIM_B:(hd + 1) * QK_DIM_B], qbg_ref[...])
        k_nope = kvup[hd * per_head:hd * per_head + QK_NOPE_DIM]
        kb_ref[hd] = norm_rope(jnp.concatenate([k_nope, kr], axis=0), kbg_ref[...])
        v = kvup[hd * per_head + QK_NOPE_DIM:(hd + 1) * per_head]
        vb_ref[hd // 2, (hd % 2) * V_DIM_B:(hd % 2 + 1) * V_DIM_B, :] = v.astype(BF16)


def _mix_proj(x, g, wa, wm, cqn, wuq, ckvn, wukv, qag, kag, qbg, kbg, cos_t, sin_t):
    t = x.shape[0]
    tm = TOKEN_TILE

    def row(w):
        return pl.BlockSpec((tm, w), lambda i: (i, 0))

    def cols(*lead):
        return pl.BlockSpec((*lead, tm), lambda i: (*(0,) * len(lead), i))

    qk_shape = jax.ShapeDtypeStruct((N_HEADS_B, QK_DIM_B, t), BF16)
    return pl.pallas_call(
        _mix_proj_kernel,
        grid=(t // tm,),
        in_specs=[row(D_MODEL), _const_spec((1, D_MODEL)),
                  _const_spec((D_MODEL, SWA_IN)), _const_spec((MLA_IN, D_MODEL)),
                  _const_spec((Q_LORA_RANK, 1)), _const_spec((N_HEADS_B * QK_DIM_B, Q_LORA_RANK)),
                  _const_spec((KV_LORA_RANK, 1)),
                  _const_spec((N_HEADS_B * (QK_NOPE_DIM + V_DIM_B), KV_LORA_RANK)),
                  _const_spec((1, LANES)), _const_spec((1, LANES)),
                  _const_spec((QK_DIM_B, 1)), _const_spec((QK_DIM_B, 1)),
                  cols(ROPE_HALF), cols(ROPE_HALF)],
        out_specs=[row(WIDTH_A), row(LANES), row(LANES),
                   cols(N_HEADS_B, QK_DIM_B), cols(N_HEADS_B, QK_DIM_B), cols(N_HEADS_B // 2, LANES)],
        out_shape=[jax.ShapeDtypeStruct((t, w), BF16) for w in (WIDTH_A, LANES, LANES)]
        + [qk_shape, qk_shape, jax.ShapeDtypeStruct((N_HEADS_B // 2, LANES, t), BF16)],
        compiler_params=_params(1),
        name="mix_proj",
    )(x, g, wa, wm, cqn, wuq, ckvn, wukv, qag, kag, qbg, kbg, cos_t, sin_t)


SWA_BLOCKS_PER_STEP = 8


def _swa_blocks(step, sink_ref, q_ref, k_ref, v_ref, pq_ref, pk_ref, o_ref):
    blk = WINDOW
    lane = lax.broadcasted_iota(jnp.int32, (1, LANES), 1)
    lo = lane < HALF
    qi = lax.broadcasted_iota(jnp.int32, (blk, 2 * blk), 0) + blk
    ki = lax.broadcasted_iota(jnp.int32, (blk, 2 * blk), 1)
    diff = qi - ki
    band = (diff >= 0) & (diff < WINDOW)
    zero = jnp.zeros((), BF16)

    def block(r):
        cur = step * SWA_BLOCKS_PER_STEP + r
        rows = slice(r * blk, (r + 1) * blk)
        if r == 0:
            prev = jnp.maximum(cur - 1, 0)
            p0 = pl.multiple_of(prev * blk, blk)
            c0 = pl.multiple_of(cur * blk, blk)
            kk = jnp.concatenate([k_ref[0, pl.ds(p0, blk), :], k_ref[0, pl.ds(c0, blk), :]], axis=0)
            vv = jnp.concatenate([v_ref[0, pl.ds(p0, blk), :], v_ref[0, pl.ds(c0, blk), :]], axis=0)
            valid = band & ((ki >= blk) | (step > 0))
        else:
            prev = cur - 1
            p0 = pl.multiple_of(prev * blk, blk)
            kk = k_ref[0, pl.ds(p0, 2 * blk), :]
            vv = v_ref[0, pl.ds(p0, 2 * blk), :]
            valid = band
        pk = jnp.concatenate([pk_ref[0, prev], pk_ref[0, cur]], axis=1)
        dist = jnp.where(valid, (pq_ref[0, rows, :] - pk).astype(F32), MASK_DIST)

        o_kv, sink_kv = [], []
        for kv in range(N_KV_HEADS_A):
            keep = lo if kv == 0 else jnp.logical_not(lo)
            qs = [jnp.where(keep, q_ref[0, rows, j * LANES:(j + 1) * LANES], zero) for j in range(GROUP_A)]
            s_all = _dot_nt(jnp.concatenate(qs, axis=0), kk)
            ps, sink_terms = [], []
            for j in range(GROUP_A):
                hd = kv * GROUP_A + j
                slope = 2.0 ** (-8.0 * (hd + 1) / N_HEADS_A)
                sink = sink_ref[hd]
                s = s_all[j * blk:(j + 1) * blk] - slope * dist
                m = jnp.maximum(jnp.max(s, axis=-1, keepdims=True), sink)
                ps.append(jnp.exp(s - m).astype(BF16))
                sink_terms.append(jnp.exp(sink - m))
            v_ext = jnp.where(keep, vv, jnp.ones((), BF16))
            o_kv.append(_dot(jnp.concatenate(ps, axis=0), v_ext))
            sink_kv.append(sink_terms)
        for j in range(GROUP_A):
            o0 = o_kv[0][j * blk:(j + 1) * blk]
            o1 = o_kv[1][j * blk:(j + 1) * blk]
            num = jnp.where(lo, o0, o1)
            den = pltpu.roll(jnp.where(lo, o1, o0), HALF, 1) + jnp.where(lo, sink_kv[0][j], sink_kv[1][j])
            o_ref[0, rows, j * LANES:(j + 1) * LANES] = (num / den).astype(BF16)

    return [functools.partial(block, r) for r in range(SWA_BLOCKS_PER_STEP)]


def _mla_units(q_ref, k_ref, v_ref, o_ref):
    s_len = q_ref.shape[2]
    tq = ATTN_BLOCK_B
    c = QK_DIM_B ** -0.5 * np.log2(np.e)
    causal = (lax.broadcasted_iota(jnp.int32, (tq, tq), 0)
              >= lax.broadcasted_iota(jnp.int32, (tq, tq), 1))
    ones = jnp.ones((V_DIM_B, s_len), BF16)
    lo = lax.broadcasted_iota(jnp.int32, (1, LANES), 1) < HALF

    v_exts = []
    for pair in range(MLA_PAIRS_PER_STEP):
        v = v_ref[pair]
        v_exts.append((jnp.concatenate([v[0:V_DIM_B], ones], axis=0),
                       jnp.concatenate([ones, v[V_DIM_B:]], axis=0)))

    def unit(pair, i):
        n = (i + 1) * tq
        o = []
        for half in range(2):
            hd = 2 * pair + half
            q = q_ref[hd, :, i * tq:n].T
            s = _dot(q, k_ref[hd, :, 0:n])
            diag = jnp.where(causal, s[:, n - tq:], NEG)
            s = diag if i == 0 else jnp.concatenate([s[:, :n - tq], diag], axis=1)
            m = jnp.max(s, axis=-1, keepdims=True)
            p = jnp.exp2((s - m) * c).astype(BF16)
            o.append(_dot_nt(p, v_exts[pair][half][:, 0:n]))
        num = jnp.where(lo, o[0], o[1])
        den = pltpu.roll(jnp.where(lo, o[1], o[0]), HALF, 1)
        o_ref[0, i * tq:n, pair * LANES:(pair + 1) * LANES] = (num / den).astype(BF16)

    return [functools.partial(unit, pair, i)
            for i in reversed(range(s_len // tq)) for pair in range(MLA_PAIRS_PER_STEP)]


def _attn_kernel(sink_ref, qa_ref, ka_ref, va_ref, pq_ref, pk_ref, qb_ref, kb_ref, vb_ref, oa_ref, ob_ref):
    swa = _swa_blocks(pl.program_id(1), sink_ref, qa_ref, ka_ref, va_ref, pq_ref, pk_ref, oa_ref)
    mla = _mla_units(qb_ref, kb_ref, vb_ref, ob_ref)
    for k in range(max(len(swa), len(mla))):
        for items in (mla, swa):
            if k < len(items):
                items[k]()


def _attention(sinks, qa, ka, va, pos_col, pos_blocks, qb, kb, vb):
    b, s, _ = qa.shape
    rows = SWA_BLOCKS_PER_STEP * WINDOW
    pp = MLA_PAIRS_PER_STEP
    assert s // rows == N_HEADS_B // (2 * pp)
    cur = lambda bi, j: (bi, j, 0)
    whole = lambda bi, j: (bi, 0, 0)
    qk_spec = pl.BlockSpec((2 * pp, QK_DIM_B, s), lambda bi, j: (j, 0, bi))
    out = jax.ShapeDtypeStruct((b, s, WIDTH_A), BF16)
    return pl.pallas_call(
        _attn_kernel,
        grid=(b, s // rows),
        in_specs=[pl.BlockSpec(memory_space=pltpu.SMEM),
                  pl.BlockSpec((1, rows, WIDTH_A), cur),
                  pl.BlockSpec((1, s, LANES), whole), pl.BlockSpec((1, s, LANES), whole),
                  pl.BlockSpec((1, rows, 1), cur),
                  pl.BlockSpec((1, s // WINDOW, 1, WINDOW), lambda bi, j: (bi, 0, 0, 0)),
                  qk_spec, qk_spec, pl.BlockSpec((pp, LANES, s), lambda bi, j: (j, 0, bi))],
        out_specs=[pl.BlockSpec((1, rows, WIDTH_A), cur),
                   pl.BlockSpec((1, s, pp * LANES), lambda bi, j: (bi, 0, j))],
        out_shape=[out, out],
        compiler_params=_params(2),
        name="attention",
    )(sinks, qa, ka, va, pos_col, pos_blocks, qb, kb, vb)


def _merge_kernel(x_ref, g_ref, oa_ref, ob_ref, wg_ref, wba_ref, wbb_ref, wo_ref, o_ref):
    x = x_ref[...]
    h = _rms(x, g_ref[...], D_MODEL).astype(BF16)
    gates = _dot(h, wg_ref[...])
    ma = _dot(oa_ref[...], wba_ref[...])
    mb = _dot(ob_ref[...], wbb_ref[...])
    merged = jax.nn.sigmoid(gates[:, :D_MODEL]) * ma + jax.nn.sigmoid(gates[:, D_MODEL:]) * mb
    o_ref[...] = x + _dot(merged.astype(BF16), wo_ref[...])


def _merge(x, g, oa, ob, wg, wba, wbb, wo):
    t = x.shape[0]
    tm = TOKEN_TILE

    def row(w):
        return pl.BlockSpec((tm, w), lambda i: (i, 0))

    return pl.pallas_call(
        _merge_kernel,
        grid=(t // tm,),
        in_specs=[row(D_MODEL), _const_spec((1, D_MODEL)), row(WIDTH_A), row(WIDTH_B),
                  _const_spec((D_MODEL, 2 * D_MODEL)), _const_spec((WIDTH_A, D_MODEL)),
                  _const_spec((WIDTH_B, D_MODEL)), _const_spec((D_MODEL, D_MODEL))],
        out_specs=row(D_MODEL),
        out_shape=jax.ShapeDtypeStruct((t, D_MODEL), F32),
        compiler_params=_params(1),
        name="merge",
    )(x, g, oa, ob, wg, wba, wbb, wo)


def kernel(x, positions, ffn1_norm, ffn1_w_gate, ffn1_w_up, ffn1_w_down, mix_norm, w_in, swa_q_norm, swa_k_norm, swa_sinks, mla_q_lora_norm, mla_w_uq, mla_kv_lora_norm, mla_w_ukv, mla_q_norm, mla_k_norm, w_branch_a, w_branch_b, w_out, ffn2_norm, ffn2_w_gate, ffn2_w_up, ffn2_w_down):
    b, s, d = x.shape
    t = b * s
    depth = w_in.shape[0]
    assert d == D_MODEL and s % ATTN_BLOCK_B == 0 and t % TOKEN_TILE == 0 and t % min(ROPE_TILE, t) == 0
    assert s % (SWA_BLOCKS_PER_STEP * WINDOW) == 0

    cos_t, sin_t = _rope_tables(positions.reshape(1, t))
    pos_col3 = positions.reshape(b, s, 1)
    pos_blocks = positions.reshape(b, s // WINDOW, 1, WINDOW)

    ffn1_stack = (ffn1_w_gate, ffn1_w_up, ffn1_w_down)
    ffn2_stack = (ffn2_w_gate, ffn2_w_up, ffn2_w_down)
    ffn_w = tuple(w[0].astype(BF16) for w in ffn1_stack)
    head_rows = WIDTH_A // len(_SWA_HEAD_ORDER)

    def swa_cols(v):
        heads = [v[:, OFF_QA + h * HEAD_DIM_A:OFF_QA + (h + 1) * HEAD_DIM_A] for h in _SWA_HEAD_ORDER]
        return jnp.concatenate(heads + [v[:, OFF_KA:OFF_CQ]], axis=1)

    def mla_rows(v):
        wide = -(-MLA_IN // LANES) * LANES
        return v[:, OFF_CQ:OFF_CQ + wide].T[:MLA_IN]

    xt = x.reshape(t, d)
    for l in range(depth):
        def mixer_jobs(n_steps, l=l):
            return ([_cast_job(w, l, n_steps) for w in ffn2_stack]
                    + [_cast_job(w_in, l, n_steps, swa_cols, SWA_IN),
                       _cast_job(w_in, l, n_steps, mla_rows, MLA_IN, transposed=True),
                       _cast_job(w_in, l, n_steps, lambda v: v[:, OFF_GA:], 2 * D_MODEL),
                       _cast_job(w_branch_a, l, n_steps, blk=head_rows,
                                 row_block=lambda r: r // 2 + GROUP_A * (r % 2)),
                       _cast_job(w_branch_b, l, n_steps), _cast_job(w_out, l, n_steps)])

        xt, (*ffn_w, wa, wm, w_gates, wba, wbb, wo) = _ffn(xt, ffn1_norm[l].reshape(1, d), *ffn_w,
                                                           make_jobs=mixer_jobs)
        qag = (jnp.tile(swa_q_norm[l], 2) * HEAD_DIM_A ** -0.5).reshape(1, LANES)
        kag = jnp.tile(swa_k_norm[l], 2).reshape(1, LANES)
        qa, ka, va, qb, kb, vb = _mix_proj(
            xt, mix_norm[l].reshape(1, d), wa, wm,
            mla_q_lora_norm[l].reshape(-1, 1), mla_w_uq[l].T.astype(BF16),
            mla_kv_lora_norm[l].reshape(-1, 1), mla_w_ukv[l].T.astype(BF16), qag, kag,
            mla_q_norm[l].reshape(-1, 1), mla_k_norm[l].reshape(-1, 1), cos_t, sin_t)

        oa, ob = _attention(swa_sinks[l], qa.reshape(b, s, -1), ka.reshape(b, s, -1), va.reshape(b, s, -1),
                            pos_col3, pos_blocks, qb, kb, vb)

        xt = _merge(xt, mix_norm[l].reshape(1, d), oa.reshape(t, -1), ob.reshape(t, -1),
                    w_gates, wba, wbb, wo)

        def next_ffn_jobs(n_steps, l=l):
            return [_cast_job(w, l + 1, n_steps) for w in ffn1_stack] if l + 1 < depth else []

        xt, ffn_w = _ffn(xt, ffn2_norm[l].reshape(1, d), *ffn_w, make_jobs=next_ffn_jobs)
    return xt.reshape(b, s, d)
```

```python
import functools
from typing import NamedTuple

import numpy as np
import jax
import jax.numpy as jnp
from jax import lax
from jax.experimental import pallas as pl
from jax.experimental.pallas import tpu as pltpu

F32 = jnp.float32
BF16 = jnp.bfloat16

D_MODEL = 1024
D_FF = 2816
HEAD_DIM_A = 64
N_HEADS_A = 8
N_KV_HEADS_A = 2
GROUP_A = N_HEADS_A // N_KV_HEADS_A
WINDOW = 128
N_HEADS_B = 8
Q_LORA_RANK = 256
KV_LORA_RANK = 128
QK_NOPE_DIM = 64
QK_ROPE_DIM = 32
QK_DIM_B = QK_NOPE_DIM + QK_ROPE_DIM
V_DIM_B = 64
ROPE_BASE = 10000.0
WIDTH_A = N_HEADS_A * HEAD_DIM_A
WIDTH_B = N_HEADS_B * V_DIM_B
EPS = 1e-6
NEG = -1e30
MASK_DIST = 1e30

LANES = 128
HALF = LANES // 2
BF16_SUBLANES = 16

OFF_QA = 0
OFF_KA = OFF_QA + WIDTH_A
OFF_VA = OFF_KA + N_KV_HEADS_A * HEAD_DIM_A
OFF_CQ = OFF_VA + N_KV_HEADS_A * HEAD_DIM_A
OFF_CKV = OFF_CQ + Q_LORA_RANK
OFF_KR = OFF_CKV + KV_LORA_RANK
OFF_GA = OFF_KR + QK_ROPE_DIM
OFF_GB = OFF_GA + D_MODEL
IN_WIDTH = OFF_GB + D_MODEL

TOKEN_TILE = 1024
FF_CHUNK = 256
ATTN_BLOCK_B = 512
MLA_PAIRS_PER_STEP = 2
ROPE_TILE = 4096
VMEM_LIMIT = 56 * 1024 * 1024

ROPE_HALF = QK_ROPE_DIM // 2
SWA_IN = OFF_CQ
MLA_IN = Q_LORA_RANK + KV_LORA_RANK + QK_ROPE_DIM

_SWA_HEAD_ORDER = np.array([h for j in range(GROUP_A) for h in (j, j + GROUP_A)])


def _rms(x, g, n):
    ssq = jnp.sum(x * x, axis=-1, keepdims=True)
    return x * lax.rsqrt(ssq / n + EPS) * g


def _rms_rows(c, g, n):
    ssq = jnp.sum(c * c, axis=0, keepdims=True)
    return c * lax.rsqrt(ssq / n + EPS) * g


def _dot(a, b):
    return jnp.dot(a, b, preferred_element_type=F32)


def _dot_nt(a, b):
    return lax.dot_general(a, b, (((1,), (1,)), ((), ())), preferred_element_type=F32)


def _const_spec(shape):
    return pl.BlockSpec(shape, lambda *_: (0,) * len(shape), pipeline_mode=pl.Buffered(1))


def _params(n_axes):
    return pltpu.CompilerParams(dimension_semantics=("parallel",) * n_axes,
                                vmem_limit_bytes=VMEM_LIMIT)


class _CastJob(NamedTuple):
    src: jax.Array
    in_spec: pl.BlockSpec
    out_spec: pl.BlockSpec
    out_shape: tuple


def _cast_job(src, layer, n_steps, row0=0, n_rows=None, blk=None, row_block=lambda r: r):
    _, rows, cols = src.shape
    n_rows = rows - row0 if n_rows is None else n_rows
    if blk is None:
        blk = next(r for r in range(BF16_SUBLANES, n_rows + 1, BF16_SUBLANES)
                   if n_rows % r == 0 and row0 % r == 0 and n_rows // r <= n_steps)
    assert n_rows % blk == 0 and row0 % blk == 0 and n_rows // blk <= n_steps
    last = n_rows // blk - 1
    first = row0 // blk
    return _CastJob(src,
                    pl.BlockSpec((None, blk, cols),
                                 lambda i: (layer, first + row_block(jnp.minimum(i, last)), 0)),
                    pl.BlockSpec((blk, cols), lambda i: (jnp.minimum(i, last), 0)),
                    (n_rows, cols))


def _ffn_kernel(*refs, n_jobs):
    x_ref, g_ref, wg_ref, wu_ref, wd_ref = refs[:5]
    o_ref = refs[5 + n_jobs]
    x = x_ref[...]
    h = _rms(x, g_ref[...], D_MODEL).astype(BF16)
    y = None
    for c in range(D_FF // FF_CHUNK):
        sl = slice(c * FF_CHUNK, (c + 1) * FF_CHUNK)
        gate = _dot(h, wg_ref[:, sl])
        up = _dot(h, wu_ref[:, sl])
        act = (gate * jax.nn.sigmoid(gate) * up).astype(BF16)
        d = _dot(act, wd_ref[sl, :])
        y = d if y is None else y + d
    o_ref[...] = x + 0.5 * y
    for src_ref, dst_ref in zip(refs[5:5 + n_jobs], refs[6 + n_jobs:]):
        dst_ref[...] = src_ref[...].astype(BF16)


def _ffn(x, g, wg, wu, wd, make_jobs=lambda n_steps: ()):
    t = x.shape[0]
    tm = TOKEN_TILE
    n_steps = t // tm
    jobs = tuple(make_jobs(n_steps))
    row = pl.BlockSpec((tm, D_MODEL), lambda i: (i, 0))
    out = pl.pallas_call(
        functools.partial(_ffn_kernel, n_jobs=len(jobs)),
        grid=(n_steps,),
        in_specs=[row, _const_spec((1, D_MODEL)), _const_spec((D_MODEL, D_FF)),
                  _const_spec((D_MODEL, D_FF)), _const_spec((D_FF, D_MODEL))] + [j.in_spec for j in jobs],
        out_specs=[row] + [j.out_spec for j in jobs],
        out_shape=[jax.ShapeDtypeStruct((t, D_MODEL), F32)]
        + [jax.ShapeDtypeStruct(j.out_shape, BF16) for j in jobs],
        compiler_params=pltpu.CompilerParams(dimension_semantics=("arbitrary",),
                                             vmem_limit_bytes=VMEM_LIMIT),
        name="ffn",
    )(x, g, wg, wu, wd, *[j.src for j in jobs])
    return out[0], tuple(out[1:])


def _rope_kernel(pos_ref, invf_ref, c_ref, s_ref):
    ang = pos_ref[...].astype(F32) * invf_ref[...]
    c_ref[...] = jnp.cos(ang)
    s_ref[...] = jnp.sin(ang)


def _rope_tables(pos_row):
    t = pos_row.shape[1]
    tm = min(ROPE_TILE, t)
    inv_freq = ROPE_BASE ** (-jnp.arange(ROPE_HALF, dtype=F32) / ROPE_HALF)
    out = jax.ShapeDtypeStruct((ROPE_HALF, t), F32)
    return pl.pallas_call(
        _rope_kernel,
        grid=(t // tm,),
        in_specs=[pl.BlockSpec((1, tm), lambda i: (0, i)), _const_spec((ROPE_HALF, 1))],
        out_specs=[pl.BlockSpec((ROPE_HALF, tm), lambda i: (0, i))] * 2,
        out_shape=[out, out],
        compiler_params=_params(1),
        name="rope_tables",
    )(pos_row, inv_freq.reshape(ROPE_HALF, 1))


def _mix_proj_kernel(x_ref, g_ref, wa_ref, wm_ref, cqn_ref, wuq_ref, ckvn_ref, wukv_ref,
                     qag_ref, kag_ref, qbg_ref, kbg_ref, cos_ref, sin_ref,
                     qa_ref, ka_ref, va_ref, qb_ref, kb_ref, vb_ref):
    x = x_ref[...]
    h = _rms(x, g_ref[...], D_MODEL).astype(BF16)

    pa = _dot_nt(h, wa_ref[...])
    lane = lax.broadcasted_iota(jnp.int32, (1, LANES), 1)
    lo = lane < HALF

    def half_norm(c, gain):
        sq = c * c
        s_lo = jnp.sum(jnp.where(lo, sq, 0.0), axis=-1, keepdims=True)
        s_hi = jnp.sum(jnp.where(lo, 0.0, sq), axis=-1, keepdims=True)
        inv = jnp.where(lo, lax.rsqrt(s_lo / HEAD_DIM_A + EPS), lax.rsqrt(s_hi / HEAD_DIM_A + EPS))
        return c * inv * gain

    for j in range(WIDTH_A // LANES):
        c = pa[:, OFF_QA + j * LANES: OFF_QA + (j + 1) * LANES]
        qa_ref[:, j * LANES:(j + 1) * LANES] = half_norm(c, qag_ref[...]).astype(BF16)
    ka_ref[...] = half_norm(pa[:, OFF_KA:OFF_VA], kag_ref[...]).astype(BF16)
    va_ref[...] = pa[:, OFF_VA:OFF_CQ].astype(BF16)

    pm = _dot_nt(wm_ref[...], h)
    cq = _rms_rows(pm[0:Q_LORA_RANK], cqn_ref[...], Q_LORA_RANK).astype(BF16)
    ckv = _rms_rows(pm[Q_LORA_RANK:Q_LORA_RANK + KV_LORA_RANK], ckvn_ref[...], KV_LORA_RANK).astype(BF16)
    kr = pm[Q_LORA_RANK + KV_LORA_RANK:MLA_IN]
    qup = _dot(wuq_ref[...], cq)
    kvup = _dot(wukv_ref[...], ckv)
    cos = cos_ref[...]
    sin = sin_ref[...]

    def norm_rope(c, gain):
        y = _rms_rows(c, gain, QK_DIM_B)
        y1 = y[QK_NOPE_DIM:QK_NOPE_DIM + ROPE_HALF]
        y2 = y[QK_NOPE_DIM + ROPE_HALF:QK_DIM_B]
        return jnp.concatenate([y[0:QK_NOPE_DIM], y1 * cos - y2 * sin, y2 * cos + y1 * sin],
                               axis=0).astype(BF16)

    per_head = QK_NOPE_DIM + V_DIM_B
    for hd in range(N_HEADS_B):
        qb_ref[hd] = norm_rope(qup[hd * QK_DIM_B:(hd + 1) * QK_DIM_B], qbg_ref[...])
        k_nope = kvup[hd * per_head:hd * per_head + QK_NOPE_DIM]
        kb_ref[hd] = norm_rope(jnp.concatenate([k_nope, kr], axis=0), kbg_ref[...])
        v = kvup[hd * per_head + QK_NOPE_DIM:(hd + 1) * per_head]
        vb_ref[hd // 2, (hd % 2) * V_DIM_B:(hd % 2 + 1) * V_DIM_B, :] = v.astype(BF16)


def _mix_proj(x, g, wa, wm, cqn, wuq, ckvn, wukv, qag, kag, qbg, kbg, cos_t, sin_t):
    t = x.shape[0]
    tm = TOKEN_TILE

    def row(w):
        return pl.BlockSpec((tm, w), lambda i: (i, 0))

    def cols(*lead):
        return pl.BlockSpec((*lead, tm), lambda i: (*(0,) * len(lead), i))

    qk_shape = jax.ShapeDtypeStruct((N_HEADS_B, QK_DIM_B, t), BF16)
    return pl.pallas_call(
        _mix_proj_kernel,
        grid=(t // tm,),
        in_specs=[row(D_MODEL), _const_spec((1, D_MODEL)),
                  _const_spec((SWA_IN, D_MODEL)), _const_spec((MLA_IN, D_MODEL)),
                  _const_spec((Q_LORA_RANK, 1)), _const_spec((N_HEADS_B * QK_DIM_B, Q_LORA_RANK)),
                  _const_spec((KV_LORA_RANK, 1)),
                  _const_spec((N_HEADS_B * (QK_NOPE_DIM + V_DIM_B), KV_LORA_RANK)),
                  _const_spec((1, LANES)), _const_spec((1, LANES)),
                  _const_spec((QK_DIM_B, 1)), _const_spec((QK_DIM_B, 1)),
                  cols(ROPE_HALF), cols(ROPE_HALF)],
        out_specs=[row(WIDTH_A), row(LANES), row(LANES),
                   cols(N_HEADS_B, QK_DIM_B), cols(N_HEADS_B, QK_DIM_B), cols(N_HEADS_B // 2, LANES)],
        out_shape=[jax.ShapeDtypeStruct((t, w), BF16) for w in (WIDTH_A, LANES, LANES)]
        + [qk_shape, qk_shape, jax.ShapeDtypeStruct((N_HEADS_B // 2, LANES, t), BF16)],
        compiler_params=_params(1),
        name="mix_proj",
    )(x, g, wa, wm, cqn, wuq, ckvn, wukv, qag, kag, qbg, kbg, cos_t, sin_t)


SWA_BLOCKS_PER_STEP = 8


def _swa_blocks(step, sink_ref, q_ref, k_ref, v_ref, pq_ref, pk_ref, o_ref):
    blk = WINDOW
    lane = lax.broadcasted_iota(jnp.int32, (1, LANES), 1)
    lo = lane < HALF
    qi = lax.broadcasted_iota(jnp.int32, (blk, 2 * blk), 0) + blk
    ki = lax.broadcasted_iota(jnp.int32, (blk, 2 * blk), 1)
    diff = qi - ki
    band = (diff >= 0) & (diff < WINDOW)
    zero = jnp.zeros((), BF16)

    def block(r):
        cur = step * SWA_BLOCKS_PER_STEP + r
        rows = slice(r * blk, (r + 1) * blk)
        if r == 0:
            prev = jnp.maximum(cur - 1, 0)
            p0 = pl.multiple_of(prev * blk, blk)
            c0 = pl.multiple_of(cur * blk, blk)
            kk = jnp.concatenate([k_ref[0, pl.ds(p0, blk), :], k_ref[0, pl.ds(c0, blk), :]], axis=0)
            vv = jnp.concatenate([v_ref[0, pl.ds(p0, blk), :], v_ref[0, pl.ds(c0, blk), :]], axis=0)
            valid = band & ((ki >= blk) | (step > 0))
        else:
            prev = cur - 1
            p0 = pl.multiple_of(prev * blk, blk)
            kk = k_ref[0, pl.ds(p0, 2 * blk), :]
            vv = v_ref[0, pl.ds(p0, 2 * blk), :]
            valid = band
        pk = jnp.concatenate([pk_ref[0, prev], pk_ref[0, cur]], axis=1)
        dist = jnp.where(valid, (pq_ref[0, rows, :] - pk).astype(F32), MASK_DIST)

        o_kv, sink_kv = [], []
        for kv in range(N_KV_HEADS_A):
            keep = lo if kv == 0 else jnp.logical_not(lo)
            qs = [jnp.where(keep, q_ref[0, rows, j * LANES:(j + 1) * LANES], zero) for j in range(GROUP_A)]
            s_all = _dot_nt(jnp.concatenate(qs, axis=0), kk)
            ps, sink_terms = [], []
            for j in range(GROUP_A):
                hd = kv * GROUP_A + j
                slope = 2.0 ** (-8.0 * (hd + 1) / N_HEADS_A)
                sink = sink_ref[hd]
                s = s_all[j * blk:(j + 1) * blk] - slope * dist
                m = jnp.maximum(jnp.max(s, axis=-1, keepdims=True), sink)
                ps.append(jnp.exp(s - m).astype(BF16))
                sink_terms.append(jnp.exp(sink - m))
            v_ext = jnp.where(keep, vv, jnp.ones((), BF16))
            o_kv.append(_dot(jnp.concatenate(ps, axis=0), v_ext))
            sink_kv.append(sink_terms)
        for j in range(GROUP_A):
            o0 = o_kv[0][j * blk:(j + 1) * blk]
            o1 = o_kv[1][j * blk:(j + 1) * blk]
            num = jnp.where(lo, o0, o1)
            den = pltpu.roll(jnp.where(lo, o1, o0), HALF, 1) + jnp.where(lo, sink_kv[0][j], sink_kv[1][j])
            o_ref[0, rows, j * LANES:(j + 1) * LANES] = (num / den).astype(BF16)

    return [functools.partial(block, r) for r in range(SWA_BLOCKS_PER_STEP)]


def _mla_units(q_ref, k_ref, v_ref, o_ref):
    s_len = q_ref.shape[2]
    tq = ATTN_BLOCK_B
    c = QK_DIM_B ** -0.5 * np.log2(np.e)
    causal = (lax.broadcasted_iota(jnp.int32, (tq, tq), 0)
              >= lax.broadcasted_iota(jnp.int32, (tq, tq), 1))
    ones = jnp.ones((V_DIM_B, s_len), BF16)
    lo = lax.broadcasted_iota(jnp.int32, (1, LANES), 1) < HALF

    v_exts = []
    for pair in range(MLA_PAIRS_PER_STEP):
        v = v_ref[pair]
        v_exts.append((jnp.concatenate([v[0:V_DIM_B], ones], axis=0),
                       jnp.concatenate([ones, v[V_DIM_B:]], axis=0)))

    def unit(pair, i):
        n = (i + 1) * tq
        o = []
        for half in range(2):
            hd = 2 * pair + half
            q = q_ref[hd, :, i * tq:n].T
            s = _dot(q, k_ref[hd, :, 0:n])
            diag = jnp.where(causal, s[:, n - tq:], NEG)
            s = diag if i == 0 else jnp.concatenate([s[:, :n - tq], diag], axis=1)
            m = jnp.max(s, axis=-1, keepdims=True)
            p = jnp.exp2((s - m) * c).astype(BF16)
            o.append(_dot_nt(p, v_exts[pair][half][:, 0:n]))
        num = jnp.where(lo, o[0], o[1])
        den = pltpu.roll(jnp.where(lo, o[1], o[0]), HALF, 1)
        o_ref[0, i * tq:n, pair * LANES:(pair + 1) * LANES] = (num / den).astype(BF16)

    return [functools.partial(unit, pair, i)
            for i in reversed(range(s_len // tq)) for pair in range(MLA_PAIRS_PER_STEP)]


def _attn_kernel(sink_ref, qa_ref, ka_ref, va_ref, pq_ref, pk_ref, qb_ref, kb_ref, vb_ref, oa_ref, ob_ref):
    swa = _swa_blocks(pl.program_id(1), sink_ref, qa_ref, ka_ref, va_ref, pq_ref, pk_ref, oa_ref)
    mla = _mla_units(qb_ref, kb_ref, vb_ref, ob_ref)
    for k in range(max(len(swa), len(mla))):
        for items in (mla, swa):
            if k < len(items):
                items[k]()


def _attention(sinks, qa, ka, va, pos_col, pos_blocks, qb, kb, vb):
    b, s, _ = qa.shape
    rows = SWA_BLOCKS_PER_STEP * WINDOW
    pp = MLA_PAIRS_PER_STEP
    assert s // rows == N_HEADS_B // (2 * pp)
    cur = lambda bi, j: (bi, j, 0)
    whole = lambda bi, j: (bi, 0, 0)
    qk_spec = pl.BlockSpec((2 * pp, QK_DIM_B, s), lambda bi, j: (j, 0, bi))
    out = jax.ShapeDtypeStruct((b, s, WIDTH_A), BF16)
    return pl.pallas_call(
        _attn_kernel,
        grid=(b, s // rows),
        in_specs=[pl.BlockSpec(memory_space=pltpu.SMEM),
                  pl.BlockSpec((1, rows, WIDTH_A), cur),
                  pl.BlockSpec((1, s, LANES), whole), pl.BlockSpec((1, s, LANES), whole),
                  pl.BlockSpec((1, rows, 1), cur),
                  pl.BlockSpec((1, s // WINDOW, 1, WINDOW), lambda bi, j: (bi, 0, 0, 0)),
                  qk_spec, qk_spec, pl.BlockSpec((pp, LANES, s), lambda bi, j: (j, 0, bi))],
        out_specs=[pl.BlockSpec((1, rows, WIDTH_A), cur),
                   pl.BlockSpec((1, s, pp * LANES), lambda bi, j: (bi, 0, j))],
        out_shape=[out, out],
        compiler_params=_params(2),
        name="attention",
    )(sinks, qa, ka, va, pos_col, pos_blocks, qb, kb, vb)


def _merge_kernel(x_ref, g_ref, oa_ref, ob_ref, wga_ref, wgb_ref, wba_ref, wbb_ref, wo_ref, o_ref):
    x = x_ref[...]
    h = _rms(x, g_ref[...], D_MODEL).astype(BF16)
    ga = _dot_nt(h, wga_ref[...])
    gb = _dot_nt(h, wgb_ref[...])
    ma = _dot(oa_ref[...], wba_ref[...])
    mb = _dot(ob_ref[...], wbb_ref[...])
    merged = jax.nn.sigmoid(ga) * ma + jax.nn.sigmoid(gb) * mb
    o_ref[...] = x + _dot(merged.astype(BF16), wo_ref[...])


def _merge(x, g, oa, ob, wga, wgb, wba, wbb, wo):
    t = x.shape[0]
    tm = TOKEN_TILE

    def row(w):
        return pl.BlockSpec((tm, w), lambda i: (i, 0))

    return pl.pallas_call(
        _merge_kernel,
        grid=(t // tm,),
        in_specs=[row(D_MODEL), _const_spec((1, D_MODEL)), row(WIDTH_A), row(WIDTH_B),
                  _const_spec((D_MODEL, D_MODEL)), _const_spec((D_MODEL, D_MODEL)),
                  _const_spec((WIDTH_A, D_MODEL)), _const_spec((WIDTH_B, D_MODEL)),
                  _const_spec((D_MODEL, D_MODEL))],
        out_specs=row(D_MODEL),
        out_shape=jax.ShapeDtypeStruct((t, D_MODEL), F32),
        compiler_params=_params(1),
        name="merge",
    )(x, g, oa, ob, wga, wgb, wba, wbb, wo)


def kernel(x, positions, ffn1_norm, ffn1_w_gate, ffn1_w_up, ffn1_w_down, mix_norm, w_in, swa_q_norm, swa_k_norm, swa_sinks, mla_q_lora_norm, mla_w_uq, mla_kv_lora_norm, mla_w_ukv, mla_q_norm, mla_k_norm, w_branch_a, w_branch_b, w_out, ffn2_norm, ffn2_w_gate, ffn2_w_up, ffn2_w_down):
    b, s, d = x.shape
    t = b * s
    depth = w_in.shape[0]
    assert d == D_MODEL and s % ATTN_BLOCK_B == 0 and t % TOKEN_TILE == 0 and t % min(ROPE_TILE, t) == 0
    assert s % (SWA_BLOCKS_PER_STEP * WINDOW) == 0

    cos_t, sin_t = _rope_tables(positions.reshape(1, t))
    pos_col3 = positions.reshape(b, s, 1)
    pos_blocks = positions.reshape(b, s // WINDOW, 1, WINDOW)

    ffn1_stack = (ffn1_w_gate, ffn1_w_up, ffn1_w_down)
    ffn2_stack = (ffn2_w_gate, ffn2_w_up, ffn2_w_down)
    ffn_w = tuple(w[0].astype(BF16) for w in ffn1_stack)
    w_in_t = jnp.swapaxes(w_in, 1, 2)

    def head_slots(r):
        return jnp.where(r < N_HEADS_A, r // 2 + GROUP_A * (r % 2), r)

    xt = x.reshape(t, d)
    for l in range(depth):
        def mixer_jobs(n_steps, l=l):
            return ([_cast_job(w, l, n_steps) for w in ffn2_stack]
                    + [_cast_job(w_in_t, l, n_steps, OFF_QA, SWA_IN, blk=HEAD_DIM_A, row_block=head_slots),
                       _cast_job(w_in_t, l, n_steps, OFF_CQ, MLA_IN),
                       _cast_job(w_in_t, l, n_steps, OFF_GA, D_MODEL),
                       _cast_job(w_in_t, l, n_steps, OFF_GB, D_MODEL),
                       _cast_job(w_branch_a, l, n_steps, blk=HEAD_DIM_A, row_block=head_slots),
                       _cast_job(w_branch_b, l, n_steps), _cast_job(w_out, l, n_steps)])

        xt, (*ffn_w, wa, wm, wga, wgb, wba, wbb, wo) = _ffn(xt, ffn1_norm[l].reshape(1, d), *ffn_w,
                                                            make_jobs=mixer_jobs)
        qag = (jnp.tile(swa_q_norm[l], 2) * HEAD_DIM_A ** -0.5).reshape(1, LANES)
        kag = jnp.tile(swa_k_norm[l], 2).reshape(1, LANES)
        qa, ka, va, qb, kb, vb = _mix_proj(
            xt, mix_norm[l].reshape(1, d), wa, wm,
            mla_q_lora_norm[l].reshape(-1, 1), mla_w_uq[l].T.astype(BF16),
            mla_kv_lora_norm[l].reshape(-1, 1), mla_w_ukv[l].T.astype(BF16), qag, kag,
            mla_q_norm[l].reshape(-1, 1), mla_k_norm[l].reshape(-1, 1), cos_t, sin_t)

        oa, ob = _attention(swa_sinks[l], qa.reshape(b, s, -1), ka.reshape(b, s, -1), va.reshape(b, s, -1),
                            pos_col3, pos_blocks, qb, kb, vb)

        xt = _merge(xt, mix_norm[l].reshape(1, d), oa.reshape(t, -1), ob.reshape(t, -1),
                    wga, wgb, wba, wbb, wo)

        def next_ffn_jobs(n_steps, l=l):
            return [_cast_job(w, l + 1, n_steps) for w in ffn1_stack] if l + 1 < depth else []

        xt, ffn_w = _ffn(xt, ffn2_norm[l].reshape(1, d), *ffn_w, make_jobs=next_ffn_jobs)
    return xt.reshape(b, s, d)
```

```python
import functools
from typing import NamedTuple

import numpy as np
import jax
import jax.numpy as jnp
from jax import lax
from jax.experimental import pallas as pl
from jax.experimental.pallas import tpu as pltpu

F32 = jnp.float32
BF16 = jnp.bfloat16

D_MODEL = 1024
D_FF = 2816
HEAD_DIM_A = 64
N_HEADS_A = 8
N_KV_HEADS_A = 2
GROUP_A = N_HEADS_A // N_KV_HEADS_A
WINDOW = 128
N_HEADS_B = 8
Q_LORA_RANK = 256
KV_LORA_RANK = 128
QK_NOPE_DIM = 64
QK_ROPE_DIM = 32
QK_DIM_B = QK_NOPE_DIM + QK_ROPE_DIM
V_DIM_B = 64
ROPE_BASE = 10000.0
WIDTH_A = N_HEADS_A * HEAD_DIM_A
WIDTH_B = N_HEADS_B * V_DIM_B
EPS = 1e-6
NEG = -1e30
MASK_DIST = 1e30

LANES = 128
HALF = LANES // 2
BF16_SUBLANES = 16

OFF_QA = 0
OFF_KA = OFF_QA + WIDTH_A
OFF_VA = OFF_KA + N_KV_HEADS_A * HEAD_DIM_A
OFF_CQ = OFF_VA + N_KV_HEADS_A * HEAD_DIM_A
OFF_CKV = OFF_CQ + Q_LORA_RANK
OFF_KR = OFF_CKV + KV_LORA_RANK
OFF_GA = OFF_KR + QK_ROPE_DIM
OFF_GB = OFF_GA + D_MODEL
IN_WIDTH = OFF_GB + D_MODEL

TOKEN_TILE = 1024
PROJ_TILE = 2048
FF_CHUNK = 256
ATTN_BLOCK_B = 512
MLA_PAIRS_PER_STEP = 2
ROPE_TILE = 4096
VMEM_LIMIT = 56 * 1024 * 1024

ROPE_HALF = QK_ROPE_DIM // 2
SWA_IN = OFF_CQ
MLA_IN = Q_LORA_RANK + KV_LORA_RANK + QK_ROPE_DIM

_SWA_HEAD_ORDER = np.array([h for j in range(GROUP_A) for h in (j, j + GROUP_A)])


def _rms(x, g, n):
    ssq = jnp.sum(x * x, axis=-1, keepdims=True)
    return x * lax.rsqrt(ssq / n + EPS) * g


def _rms_rows(c, g, n):
    ssq = jnp.sum(c * c, axis=0, keepdims=True)
    return c * lax.rsqrt(ssq / n + EPS) * g


def _dot(a, b):
    return jnp.dot(a, b, preferred_element_type=F32)


def _dot_nt(a, b):
    return lax.dot_general(a, b, (((1,), (1,)), ((), ())), preferred_element_type=F32)


def _const_spec(shape):
    return pl.BlockSpec(shape, lambda *_: (0,) * len(shape), pipeline_mode=pl.Buffered(1))


def _params(n_axes):
    return pltpu.CompilerParams(dimension_semantics=("parallel",) * n_axes,
                                vmem_limit_bytes=VMEM_LIMIT)


class _CastJob(NamedTuple):
    src: jax.Array
    in_spec: pl.BlockSpec
    out_spec: pl.BlockSpec
    out_shape: tuple


def _cast_job(src, layer, n_steps, row0=0, n_rows=None, blk=None, row_block=lambda r: r):
    _, rows, cols = src.shape
    n_rows = rows - row0 if n_rows is None else n_rows
    if blk is None:
        blk = next(r for r in range(BF16_SUBLANES, n_rows + 1, BF16_SUBLANES)
                   if n_rows % r == 0 and row0 % r == 0 and n_rows // r <= n_steps)
    assert n_rows % blk == 0 and row0 % blk == 0 and n_rows // blk <= n_steps
    last = n_rows // blk - 1
    first = row0 // blk
    return _CastJob(src,
                    pl.BlockSpec((None, blk, cols),
                                 lambda i: (layer, first + row_block(jnp.minimum(i, last)), 0)),
                    pl.BlockSpec((blk, cols), lambda i: (jnp.minimum(i, last), 0)),
                    (n_rows, cols))


def _ffn_kernel(*refs, n_jobs):
    x_ref, g_ref, wg_ref, wu_ref, wd_ref = refs[:5]
    o_ref = refs[5 + n_jobs]
    x = x_ref[...]
    h = _rms(x, g_ref[...], D_MODEL).astype(BF16)
    y = None
    for c in range(D_FF // FF_CHUNK):
        sl = slice(c * FF_CHUNK, (c + 1) * FF_CHUNK)
        gate = _dot(h, wg_ref[:, sl])
        up = _dot(h, wu_ref[:, sl])
        act = (gate * jax.nn.sigmoid(gate) * up).astype(BF16)
        d = _dot(act, wd_ref[sl, :])
        y = d if y is None else y + d
    o_ref[...] = x + 0.5 * y
    for src_ref, dst_ref in zip(refs[5:5 + n_jobs], refs[6 + n_jobs:]):
        dst_ref[...] = src_ref[...].astype(BF16)


def _ffn(x, g, wg, wu, wd, make_jobs=lambda n_steps: ()):
    t = x.shape[0]
    tm = TOKEN_TILE
    n_steps = t // tm
    jobs = tuple(make_jobs(n_steps))
    row = pl.BlockSpec((tm, D_MODEL), lambda i: (i, 0))
    out = pl.pallas_call(
        functools.partial(_ffn_kernel, n_jobs=len(jobs)),
        grid=(n_steps,),
        in_specs=[row, _const_spec((1, D_MODEL)), _const_spec((D_MODEL, D_FF)),
                  _const_spec((D_MODEL, D_FF)), _const_spec((D_FF, D_MODEL))] + [j.in_spec for j in jobs],
        out_specs=[row] + [j.out_spec for j in jobs],
        out_shape=[jax.ShapeDtypeStruct((t, D_MODEL), F32)]
        + [jax.ShapeDtypeStruct(j.out_shape, BF16) for j in jobs],
        compiler_params=pltpu.CompilerParams(dimension_semantics=("arbitrary",),
                                             vmem_limit_bytes=VMEM_LIMIT),
        name="ffn",
    )(x, g, wg, wu, wd, *[j.src for j in jobs])
    return out[0], tuple(out[1:])


def _rope_kernel(pos_ref, invf_ref, c_ref, s_ref):
    ang = pos_ref[...].astype(F32) * invf_ref[...]
    c_ref[...] = jnp.cos(ang)
    s_ref[...] = jnp.sin(ang)


def _rope_tables(pos_row):
    t = pos_row.shape[1]
    tm = min(ROPE_TILE, t)
    inv_freq = ROPE_BASE ** (-jnp.arange(ROPE_HALF, dtype=F32) / ROPE_HALF)
    out = jax.ShapeDtypeStruct((ROPE_HALF, t), F32)
    return pl.pallas_call(
        _rope_kernel,
        grid=(t // tm,),
        in_specs=[pl.BlockSpec((1, tm), lambda i: (0, i)), _const_spec((ROPE_HALF, 1))],
        out_specs=[pl.BlockSpec((ROPE_HALF, tm), lambda i: (0, i))] * 2,
        out_shape=[out, out],
        compiler_params=_params(1),
        name="rope_tables",
    )(pos_row, inv_freq.reshape(ROPE_HALF, 1))


def _mix_proj_kernel(x_ref, g_ref, wa_ref, wm_ref, cqn_ref, wuq_ref, ckvn_ref, wukv_ref,
                     qag_ref, kag_ref, qbg_ref, kbg_ref, cos_ref, sin_ref,
                     qa_ref, ka_ref, va_ref, qb_ref, kb_ref, vb_ref):
    x = x_ref[...]
    h = _rms(x, g_ref[...], D_MODEL).astype(BF16)

    pa = _dot_nt(h, wa_ref[...])
    lane = lax.broadcasted_iota(jnp.int32, (1, LANES), 1)
    lo = lane < HALF

    def half_norm(c, gain):
        sq = c * c
        s_lo = jnp.sum(jnp.where(lo, sq, 0.0), axis=-1, keepdims=True)
        s_hi = jnp.sum(jnp.where(lo, 0.0, sq), axis=-1, keepdims=True)
        inv = jnp.where(lo, lax.rsqrt(s_lo / HEAD_DIM_A + EPS), lax.rsqrt(s_hi / HEAD_DIM_A + EPS))
        return c * inv * gain

    for j in range(WIDTH_A // LANES):
        c = pa[:, OFF_QA + j * LANES: OFF_QA + (j + 1) * LANES]
        qa_ref[:, j * LANES:(j + 1) * LANES] = half_norm(c, qag_ref[...]).astype(BF16)
    ka_ref[...] = half_norm(pa[:, OFF_KA:OFF_VA], kag_ref[...]).astype(BF16)
    va_ref[...] = pa[:, OFF_VA:OFF_CQ].astype(BF16)

    pm = _dot_nt(wm_ref[...], h)
    cq = _rms_rows(pm[0:Q_LORA_RANK], cqn_ref[...], Q_LORA_RANK).astype(BF16)
    ckv = _rms_rows(pm[Q_LORA_RANK:Q_LORA_RANK + KV_LORA_RANK], ckvn_ref[...], KV_LORA_RANK).astype(BF16)
    kr = pm[Q_LORA_RANK + KV_LORA_RANK:MLA_IN]
    qup = _dot(wuq_ref[...], cq)
    kvup = _dot(wukv_ref[...], ckv)
    cos = cos_ref[...]
    sin = sin_ref[...]

    def norm_rope(c, gain):
        y = _rms_rows(c, gain, QK_DIM_B)
        y1 = y[QK_NOPE_DIM:QK_NOPE_DIM + ROPE_HALF]
        y2 = y[QK_NOPE_DIM + ROPE_HALF:QK_DIM_B]
        return jnp.concatenate([y[0:QK_NOPE_DIM], y1 * cos - y2 * sin, y2 * cos + y1 * sin],
                               axis=0).astype(BF16)

    per_head = QK_NOPE_DIM + V_DIM_B
    for hd in range(N_HEADS_B):
        qb_ref[hd] = norm_rope(qup[hd * QK_DIM_B:(hd + 1) * QK_DIM_B], qbg_ref[...])
        k_nope = kvup[hd * per_head:hd * per_head + QK_NOPE_DIM]
        kb_ref[hd] = norm_rope(jnp.concatenate([k_nope, kr], axis=0), kbg_ref[...])
        v = kvup[hd * per_head + QK_NOPE_DIM:(hd + 1) * per_head]
        vb_ref[hd // 2, (hd % 2) * V_DIM_B:(hd % 2 + 1) * V_DIM_B, :] = v.astype(BF16)


def _mix_proj(x, g, wa, wm, cqn, wuq, ckvn, wukv, qag, kag, qbg, kbg, cos_t, sin_t):
    t = x.shape[0]
    tm = PROJ_TILE

    def row(w):
        return pl.BlockSpec((tm, w), lambda i: (i, 0))

    def cols(*lead):
        return pl.BlockSpec((*lead, tm), lambda i: (*(0,) * len(lead), i))

    qk_shape = jax.ShapeDtypeStruct((N_HEADS_B, QK_DIM_B, t), BF16)
    return pl.pallas_call(
        _mix_proj_kernel,
        grid=(t // tm,),
        in_specs=[row(D_MODEL), _const_spec((1, D_MODEL)),
                  _const_spec((SWA_IN, D_MODEL)), _const_spec((MLA_IN, D_MODEL)),
                  _const_spec((Q_LORA_RANK, 1)), _const_spec((N_HEADS_B * QK_DIM_B, Q_LORA_RANK)),
                  _const_spec((KV_LORA_RANK, 1)),
                  _const_spec((N_HEADS_B * (QK_NOPE_DIM + V_DIM_B), KV_LORA_RANK)),
                  _const_spec((1, LANES)), _const_spec((1, LANES)),
                  _const_spec((QK_DIM_B, 1)), _const_spec((QK_DIM_B, 1)),
                  cols(ROPE_HALF), cols(ROPE_HALF)],
        out_specs=[row(WIDTH_A), row(LANES), row(LANES),
                   cols(N_HEADS_B, QK_DIM_B), cols(N_HEADS_B, QK_DIM_B), cols(N_HEADS_B // 2, LANES)],
        out_shape=[jax.ShapeDtypeStruct((t, w), BF16) for w in (WIDTH_A, LANES, LANES)]
        + [qk_shape, qk_shape, jax.ShapeDtypeStruct((N_HEADS_B // 2, LANES, t), BF16)],
        compiler_params=_params(1),
        name="mix_proj",
    )(x, g, wa, wm, cqn, wuq, ckvn, wukv, qag, kag, qbg, kbg, cos_t, sin_t)


SWA_BLOCKS_PER_STEP = 8


def _swa_blocks(step, sink_ref, q_ref, k_ref, v_ref, pq_ref, pk_ref, o_ref):
    blk = WINDOW
    lane = lax.broadcasted_iota(jnp.int32, (1, LANES), 1)
    lo = lane < HALF
    qi = lax.broadcasted_iota(jnp.int32, (blk, 2 * blk), 0) + blk
    ki = lax.broadcasted_iota(jnp.int32, (blk, 2 * blk), 1)
    diff = qi - ki
    band = (diff >= 0) & (diff < WINDOW)
    zero = jnp.zeros((), BF16)

    def block(r):
        cur = step * SWA_BLOCKS_PER_STEP + r
        rows = slice(r * blk, (r + 1) * blk)
        if r == 0:
            prev = jnp.maximum(cur - 1, 0)
            p0 = pl.multiple_of(prev * blk, blk)
            c0 = pl.multiple_of(cur * blk, blk)
            kk = jnp.concatenate([k_ref[0, pl.ds(p0, blk), :], k_ref[0, pl.ds(c0, blk), :]], axis=0)
            vv = jnp.concatenate([v_ref[0, pl.ds(p0, blk), :], v_ref[0, pl.ds(c0, blk), :]], axis=0)
            valid = band & ((ki >= blk) | (step > 0))
        else:
            prev = cur - 1
            p0 = pl.multiple_of(prev * blk, blk)
            kk = k_ref[0, pl.ds(p0, 2 * blk), :]
            vv = v_ref[0, pl.ds(p0, 2 * blk), :]
            valid = band
        pk = jnp.concatenate([pk_ref[0, prev], pk_ref[0, cur]], axis=1)
        dist = jnp.where(valid, (pq_ref[0, rows, :] - pk).astype(F32), MASK_DIST)

        o_kv, sink_kv = [], []
        for kv in range(N_KV_HEADS_A):
            keep = lo if kv == 0 else jnp.logical_not(lo)
            qs = [jnp.where(keep, q_ref[0, rows, j * LANES:(j + 1) * LANES], zero) for j in range(GROUP_A)]
            s_all = _dot_nt(jnp.concatenate(qs, axis=0), kk)
            ps, sink_terms = [], []
            for j in range(GROUP_A):
                hd = kv * GROUP_A + j
                slope = 2.0 ** (-8.0 * (hd + 1) / N_HEADS_A)
                sink = sink_ref[hd]
                s = s_all[j * blk:(j + 1) * blk] - slope * dist
                m = jnp.maximum(jnp.max(s, axis=-1, keepdims=True), sink)
                ps.append(jnp.exp(s - m).astype(BF16))
                sink_terms.append(jnp.exp(sink - m))
            v_ext = jnp.where(keep, vv, jnp.ones((), BF16))
            o_kv.append(_dot(jnp.concatenate(ps, axis=0), v_ext))
            sink_kv.append(sink_terms)
        for j in range(GROUP_A):
            o0 = o_kv[0][j * blk:(j + 1) * blk]
            o1 = o_kv[1][j * blk:(j + 1) * blk]
            num = jnp.where(lo, o0, o1)
            den = pltpu.roll(jnp.where(lo, o1, o0), HALF, 1) + jnp.where(lo, sink_kv[0][j], sink_kv[1][j])
            o_ref[0, rows, j * LANES:(j + 1) * LANES] = (num / den).astype(BF16)

    return [functools.partial(block, r) for r in range(SWA_BLOCKS_PER_STEP)]


def _mla_units(q_ref, k_ref, v_ref, o_ref):
    s_len = q_ref.shape[2]
    tq = ATTN_BLOCK_B
    c = QK_DIM_B ** -0.5 * np.log2(np.e)
    causal = (lax.broadcasted_iota(jnp.int32, (tq, tq), 0)
              >= lax.broadcasted_iota(jnp.int32, (tq, tq), 1))
    ones = jnp.ones((V_DIM_B, s_len), BF16)
    lo = lax.broadcasted_iota(jnp.int32, (1, LANES), 1) < HALF

    v_exts = []
    for pair in range(MLA_PAIRS_PER_STEP):
        v = v_ref[pair]
        v_exts.append((jnp.concatenate([v[0:V_DIM_B], ones], axis=0),
                       jnp.concatenate([ones, v[V_DIM_B:]], axis=0)))

    def unit(pair, i):
        n = (i + 1) * tq
        o = []
        for half in range(2):
            hd = 2 * pair + half
            q = q_ref[hd, :, i * tq:n].T
            s = _dot(q, k_ref[hd, :, 0:n])
            diag = jnp.where(causal, s[:, n - tq:], NEG)
            s = diag if i == 0 else jnp.concatenate([s[:, :n - tq], diag], axis=1)
            m = jnp.max(s, axis=-1, keepdims=True)
            p = jnp.exp2((s - m) * c).astype(BF16)
            o.append(_dot_nt(p, v_exts[pair][half][:, 0:n]))
        num = jnp.where(lo, o[0], o[1])
        den = pltpu.roll(jnp.where(lo, o[1], o[0]), HALF, 1)
        o_ref[0, i * tq:n, pair * LANES:(pair + 1) * LANES] = (num / den).astype(BF16)

    return [functools.partial(unit, pair, i)
            for i in reversed(range(s_len // tq)) for pair in range(MLA_PAIRS_PER_STEP)]


def _attn_kernel(sink_ref, qa_ref, ka_ref, va_ref, pq_ref, pk_ref, qb_ref, kb_ref, vb_ref, oa_ref, ob_ref):
    swa = _swa_blocks(pl.program_id(1), sink_ref, qa_ref, ka_ref, va_ref, pq_ref, pk_ref, oa_ref)
    mla = _mla_units(qb_ref, kb_ref, vb_ref, ob_ref)
    for k in range(max(len(swa), len(mla))):
        for items in (mla, swa):
            if k < len(items):
                items[k]()


def _attention(sinks, qa, ka, va, pos_col, pos_blocks, qb, kb, vb):
    b, s, _ = qa.shape
    rows = SWA_BLOCKS_PER_STEP * WINDOW
    pp = MLA_PAIRS_PER_STEP
    assert s // rows == N_HEADS_B // (2 * pp)
    cur = lambda bi, j: (bi, j, 0)
    whole = lambda bi, j: (bi, 0, 0)
    qk_spec = pl.BlockSpec((2 * pp, QK_DIM_B, s), lambda bi, j: (j, 0, bi))
    out = jax.ShapeDtypeStruct((b, s, WIDTH_A), BF16)
    return pl.pallas_call(
        _attn_kernel,
        grid=(b, s // rows),
        in_specs=[pl.BlockSpec(memory_space=pltpu.SMEM),
                  pl.BlockSpec((1, rows, WIDTH_A), cur),
                  pl.BlockSpec((1, s, LANES), whole), pl.BlockSpec((1, s, LANES), whole),
                  pl.BlockSpec((1, rows, 1), cur),
                  pl.BlockSpec((1, s // WINDOW, 1, WINDOW), lambda bi, j: (bi, 0, 0, 0)),
                  qk_spec, qk_spec, pl.BlockSpec((pp, LANES, s), lambda bi, j: (j, 0, bi))],
        out_specs=[pl.BlockSpec((1, rows, WIDTH_A), cur),
                   pl.BlockSpec((1, s, pp * LANES), lambda bi, j: (bi, 0, j))],
        out_shape=[out, out],
        compiler_params=_params(2),
        name="attention",
    )(sinks, qa, ka, va, pos_col, pos_blocks, qb, kb, vb)


def _merge_kernel(x_ref, g_ref, oa_ref, ob_ref, wga_ref, wgb_ref, wba_ref, wbb_ref, wo_ref, o_ref):
    x = x_ref[...]
    h = _rms(x, g_ref[...], D_MODEL).astype(BF16)
    ga = _dot_nt(h, wga_ref[...])
    gb = _dot_nt(h, wgb_ref[...])
    ma = _dot(oa_ref[...], wba_ref[...])
    mb = _dot(ob_ref[...], wbb_ref[...])
    merged = jax.nn.sigmoid(ga) * ma + jax.nn.sigmoid(gb) * mb
    o_ref[...] = x + _dot(merged.astype(BF16), wo_ref[...])


def _merge(x, g, oa, ob, wga, wgb, wba, wbb, wo):
    t = x.shape[0]
    tm = TOKEN_TILE

    def row(w):
        return pl.BlockSpec((tm, w), lambda i: (i, 0))

    return pl.pallas_call(
        _merge_kernel,
        grid=(t // tm,),
        in_specs=[row(D_MODEL), _const_spec((1, D_MODEL)), row(WIDTH_A), row(WIDTH_B),
                  _const_spec((D_MODEL, D_MODEL)), _const_spec((D_MODEL, D_MODEL)),
                  _const_spec((WIDTH_A, D_MODEL)), _const_spec((WIDTH_B, D_MODEL)),
                  _const_spec((D_MODEL, D_MODEL))],
        out_specs=row(D_MODEL),
        out_shape=jax.ShapeDtypeStruct((t, D_MODEL), F32),
        compiler_params=_params(1),
        name="merge",
    )(x, g, oa, ob, wga, wgb, wba, wbb, wo)


def kernel(x, positions, ffn1_norm, ffn1_w_gate, ffn1_w_up, ffn1_w_down, mix_norm, w_in, swa_q_norm, swa_k_norm, swa_sinks, mla_q_lora_norm, mla_w_uq, mla_kv_lora_norm, mla_w_ukv, mla_q_norm, mla_k_norm, w_branch_a, w_branch_b, w_out, ffn2_norm, ffn2_w_gate, ffn2_w_up, ffn2_w_down):
    b, s, d = x.shape
    t = b * s
    depth = w_in.shape[0]
    assert d == D_MODEL and s % ATTN_BLOCK_B == 0 and t % TOKEN_TILE == 0 and t % min(ROPE_TILE, t) == 0
    assert s % (SWA_BLOCKS_PER_STEP * WINDOW) == 0 and t % PROJ_TILE == 0

    cos_t, sin_t = _rope_tables(positions.reshape(1, t))
    pos_col3 = positions.reshape(b, s, 1)
    pos_blocks = positions.reshape(b, s // WINDOW, 1, WINDOW)

    ffn1_stack = (ffn1_w_gate, ffn1_w_up, ffn1_w_down)
    ffn2_stack = (ffn2_w_gate, ffn2_w_up, ffn2_w_down)
    ffn_w = tuple(w[0].astype(BF16) for w in ffn1_stack)
    w_in_t = jnp.swapaxes(w_in, 1, 2)

    def head_slots(r):
        return jnp.where(r < N_HEADS_A, r // 2 + GROUP_A * (r % 2), r)

    xt = x.reshape(t, d)
    for l in range(depth):
        def mixer_jobs(n_steps, l=l):
            return ([_cast_job(w, l, n_steps) for w in ffn2_stack]
                    + [_cast_job(w_in_t, l, n_steps, OFF_QA, SWA_IN, blk=HEAD_DIM_A, row_block=head_slots),
                       _cast_job(w_in_t, l, n_steps, OFF_CQ, MLA_IN),
                       _cast_job(w_in_t, l, n_steps, OFF_GA, D_MODEL),
                       _cast_job(w_in_t, l, n_steps, OFF_GB, D_MODEL),
                       _cast_job(w_branch_a, l, n_steps, blk=HEAD_DIM_A, row_block=head_slots),
                       _cast_job(w_branch_b, l, n_steps), _cast_job(w_out, l, n_steps)])

        xt, (*ffn_w, wa, wm, wga, wgb, wba, wbb, wo) = _ffn(xt, ffn1_norm[l].reshape(1, d), *ffn_w,
                                                            make_jobs=mixer_jobs)
        qag = (jnp.tile(swa_q_norm[l], 2) * HEAD_DIM_A ** -0.5).reshape(1, LANES)
        kag = jnp.tile(swa_k_norm[l], 2).reshape(1, LANES)
        qa, ka, va, qb, kb, vb = _mix_proj(
            xt, mix_norm[l].reshape(1, d), wa, wm,
            mla_q_lora_norm[l].reshape(-1, 1), mla_w_uq[l].T.astype(BF16),
            mla_kv_lora_norm[l].reshape(-1, 1), mla_w_ukv[l].T.astype(BF16), qag, kag,
            mla_q_norm[l].reshape(-1, 1), mla_k_norm[l].reshape(-1, 1), cos_t, sin_t)

        oa, ob = _attention(swa_sinks[l], qa.reshape(b, s, -1), ka.reshape(b, s, -1), va.reshape(b, s, -1),
                            pos_col3, pos_blocks, qb, kb, vb)

        xt = _merge(xt, mix_norm[l].reshape(1, d), oa.reshape(t, -1), ob.reshape(t, -1),
                    wga, wgb, wba, wbb, wo)

        def next_ffn_jobs(n_steps, l=l):
            return [_cast_job(w, l + 1, n_steps) for w in ffn1_stack] if l + 1 < depth else []

        xt, ffn_w = _ffn(xt, ffn2_norm[l].reshape(1, d), *ffn_w, make_jobs=next_ffn_jobs)
    return xt.reshape(b, s, d)
```

```python
import functools
from typing import NamedTuple

import numpy as np
import jax
import jax.numpy as jnp
from jax import lax
from jax.experimental import pallas as pl
from jax.experimental.pallas import tpu as pltpu

F32 = jnp.float32
BF16 = jnp.bfloat16

D_MODEL = 1024
D_FF = 2816
HEAD_DIM_A = 64
N_HEADS_A = 8
N_KV_HEADS_A = 2
GROUP_A = N_HEADS_A // N_KV_HEADS_A
WINDOW = 128
N_HEADS_B = 8
Q_LORA_RANK = 256
KV_LORA_RANK = 128
QK_NOPE_DIM = 64
QK_ROPE_DIM = 32
QK_DIM_B = QK_NOPE_DIM + QK_ROPE_DIM
V_DIM_B = 64
ROPE_BASE = 10000.0
WIDTH_A = N_HEADS_A * HEAD_DIM_A
WIDTH_B = N_HEADS_B * V_DIM_B
EPS = 1e-6
NEG = -1e30
MASK_DIST = 1e30

LANES = 128
HALF = LANES // 2
BF16_SUBLANES = 16

OFF_QA = 0
OFF_KA = OFF_QA + WIDTH_A
OFF_VA = OFF_KA + N_KV_HEADS_A * HEAD_DIM_A
OFF_CQ = OFF_VA + N_KV_HEADS_A * HEAD_DIM_A
OFF_CKV = OFF_CQ + Q_LORA_RANK
OFF_KR = OFF_CKV + KV_LORA_RANK
OFF_GA = OFF_KR + QK_ROPE_DIM
OFF_GB = OFF_GA + D_MODEL
IN_WIDTH = OFF_GB + D_MODEL

TOKEN_TILE = 1024
FF_CHUNK = 256
ATTN_BLOCK_B = 512
MLA_PAIRS_PER_STEP = 2
ROPE_TILE = 4096
VMEM_LIMIT = 56 * 1024 * 1024

ROPE_HALF = QK_ROPE_DIM // 2
SWA_IN = OFF_CQ
MLA_IN = Q_LORA_RANK + KV_LORA_RANK + QK_ROPE_DIM

_SWA_HEAD_ORDER = np.array([h for j in range(GROUP_A) for h in (j, j + GROUP_A)])


def _rms(x, g, n):
    ssq = jnp.sum(x * x, axis=-1, keepdims=True)
    return x * lax.rsqrt(ssq / n + EPS) * g


def _rms_rows(c, g, n):
    ssq = jnp.sum(c * c, axis=0, keepdims=True)
    return c * lax.rsqrt(ssq / n + EPS) * g


def _dot(a, b):
    return jnp.dot(a, b, preferred_element_type=F32)


def _dot_nt(a, b):
    return lax.dot_general(a, b, (((1,), (1,)), ((), ())), preferred_element_type=F32)


def _const_spec(shape):
    return pl.BlockSpec(shape, lambda *_: (0,) * len(shape), pipeline_mode=pl.Buffered(1))


def _params(n_axes):
    return pltpu.CompilerParams(dimension_semantics=("parallel",) * n_axes,
                                vmem_limit_bytes=VMEM_LIMIT)


class _CastJob(NamedTuple):
    src: jax.Array
    in_spec: pl.BlockSpec
    out_spec: pl.BlockSpec
    out_shape: tuple


def _cast_job(src, layer, n_steps, row0=0, n_rows=None, blk=None, row_block=lambda r: r):
    _, rows, cols = src.shape
    n_rows = rows - row0 if n_rows is None else n_rows
    if blk is None:
        blk = next(r for r in range(BF16_SUBLANES, n_rows + 1, BF16_SUBLANES)
                   if n_rows % r == 0 and row0 % r == 0 and n_rows // r <= n_steps)
    assert n_rows % blk == 0 and row0 % blk == 0 and n_rows // blk <= n_steps
    last = n_rows // blk - 1
    first = row0 // blk
    return _CastJob(src,
                    pl.BlockSpec((None, blk, cols),
                                 lambda i: (layer, first + row_block(jnp.minimum(i, last)), 0)),
                    pl.BlockSpec((blk, cols), lambda i: (jnp.minimum(i, last), 0)),
                    (n_rows, cols))


def _ffn_kernel(*refs, n_jobs):
    x_ref, g_ref, wg_ref, wu_ref, wd_ref = refs[:5]
    o_ref = refs[5 + n_jobs]
    x = x_ref[...]
    h = _rms(x, g_ref[...], D_MODEL).astype(BF16)
    y = None
    for c in range(D_FF // FF_CHUNK):
        sl = slice(c * FF_CHUNK, (c + 1) * FF_CHUNK)
        gate = _dot(h, wg_ref[:, sl])
        up = _dot(h, wu_ref[:, sl])
        act = (gate * jax.nn.sigmoid(gate) * up).astype(BF16)
        d = _dot(act, wd_ref[sl, :])
        y = d if y is None else y + d
    o_ref[...] = x + 0.5 * y
    for src_ref, dst_ref in zip(refs[5:5 + n_jobs], refs[6 + n_jobs:]):
        dst_ref[...] = src_ref[...].astype(BF16)


def _ffn(x, g, wg, wu, wd, make_jobs=lambda n_steps: ()):
    t = x.shape[0]
    tm = TOKEN_TILE
    n_steps = t // tm
    jobs = tuple(make_jobs(n_steps))
    row = pl.BlockSpec((tm, D_MODEL), lambda i: (i, 0))
    out = pl.pallas_call(
        functools.partial(_ffn_kernel, n_jobs=len(jobs)),
        grid=(n_steps,),
        in_specs=[row, _const_spec((1, D_MODEL)), _const_spec((D_MODEL, D_FF)),
                  _const_spec((D_MODEL, D_FF)), _const_spec((D_FF, D_MODEL))] + [j.in_spec for j in jobs],
        out_specs=[row] + [j.out_spec for j in jobs],
        out_shape=[jax.ShapeDtypeStruct((t, D_MODEL), F32)]
        + [jax.ShapeDtypeStruct(j.out_shape, BF16) for j in jobs],
        compiler_params=pltpu.CompilerParams(dimension_semantics=("arbitrary",),
                                             vmem_limit_bytes=VMEM_LIMIT),
        name="ffn",
    )(x, g, wg, wu, wd, *[j.src for j in jobs])
    return out[0], tuple(out[1:])


def _rope_kernel(pos_ref, invf_ref, c_ref, s_ref):
    ang = pos_ref[...].astype(F32) * invf_ref[...]
    c_ref[...] = jnp.cos(ang)
    s_ref[...] = jnp.sin(ang)


def _rope_tables(pos_row):
    t = pos_row.shape[1]
    tm = min(ROPE_TILE, t)
    inv_freq = ROPE_BASE ** (-jnp.arange(ROPE_HALF, dtype=F32) / ROPE_HALF)
    out = jax.ShapeDtypeStruct((ROPE_HALF, t), F32)
    return pl.pallas_call(
        _rope_kernel,
        grid=(t // tm,),
        in_specs=[pl.BlockSpec((1, tm), lambda i: (0, i)), _const_spec((ROPE_HALF, 1))],
        out_specs=[pl.BlockSpec((ROPE_HALF, tm), lambda i: (0, i))] * 2,
        out_shape=[out, out],
        compiler_params=_params(1),
        name="rope_tables",
    )(pos_row, inv_freq.reshape(ROPE_HALF, 1))


def _mix_proj_kernel(x_ref, g_ref, wa_ref, wm_ref, cqn_ref, wuq_ref, ckvn_ref, wukv_ref,
                     qag_ref, kag_ref, qbg_ref, kbg_ref, cos_ref, sin_ref,
                     qa_ref, ka_ref, va_ref, qb_ref, kb_ref, vb_ref):
    x = x_ref[...]
    h = _rms(x, g_ref[...], D_MODEL).astype(BF16)

    pa = _dot_nt(h, wa_ref[...])
    lane = lax.broadcasted_iota(jnp.int32, (1, LANES), 1)
    lo = lane < HALF

    def half_norm(c, gain):
        sq = c * c
        s_lo = jnp.sum(jnp.where(lo, sq, 0.0), axis=-1, keepdims=True)
        s_hi = jnp.sum(jnp.where(lo, 0.0, sq), axis=-1, keepdims=True)
        inv = jnp.where(lo, lax.rsqrt(s_lo / HEAD_DIM_A + EPS), lax.rsqrt(s_hi / HEAD_DIM_A + EPS))
        return c * inv * gain

    for j in range(WIDTH_A // LANES):
        c = pa[:, OFF_QA + j * LANES: OFF_QA + (j + 1) * LANES]
        qa_ref[:, j * LANES:(j + 1) * LANES] = half_norm(c, qag_ref[...]).astype(BF16)
    ka_ref[...] = half_norm(pa[:, OFF_KA:OFF_VA], kag_ref[...]).astype(BF16)
    va_ref[...] = pa[:, OFF_VA:OFF_CQ].astype(BF16)

    pm = _dot_nt(wm_ref[...], h)
    cq = _rms_rows(pm[0:Q_LORA_RANK], cqn_ref[...], Q_LORA_RANK).astype(BF16)
    ckv = _rms_rows(pm[Q_LORA_RANK:Q_LORA_RANK + KV_LORA_RANK], ckvn_ref[...], KV_LORA_RANK).astype(BF16)
    kr = pm[Q_LORA_RANK + KV_LORA_RANK:MLA_IN]
    qup = _dot(wuq_ref[...], cq)
    kvup = _dot(wukv_ref[...], ckv)
    cos = cos_ref[...]
    sin = sin_ref[...]

    def norm_rope(c, gain):
        y = _rms_rows(c, gain, QK_DIM_B)
        y1 = y[QK_NOPE_DIM:QK_NOPE_DIM + ROPE_HALF]
        y2 = y[QK_NOPE_DIM + ROPE_HALF:QK_DIM_B]
        return jnp.concatenate([y[0:QK_NOPE_DIM], y1 * cos - y2 * sin, y2 * cos + y1 * sin],
                               axis=0).astype(BF16)

    per_head = QK_NOPE_DIM + V_DIM_B
    for hd in range(N_HEADS_B):
        qb_ref[hd] = norm_rope(qup[hd * QK_DIM_B:(hd + 1) * QK_DIM_B], qbg_ref[...])
        k_nope = kvup[hd * per_head:hd * per_head + QK_NOPE_DIM]
        kb_ref[hd] = norm_rope(jnp.concatenate([k_nope, kr], axis=0), kbg_ref[...])
        v = kvup[hd * per_head + QK_NOPE_DIM:(hd + 1) * per_head]
        vb_ref[hd // 2, (hd % 2) * V_DIM_B:(hd % 2 + 1) * V_DIM_B, :] = v.astype(BF16)


def _mix_proj(x, g, wa, wm, cqn, wuq, ckvn, wukv, qag, kag, qbg, kbg, cos_t, sin_t):
    t = x.shape[0]
    tm = TOKEN_TILE

    def row(w):
        return pl.BlockSpec((tm, w), lambda i: (i, 0))

    def cols(*lead):
        return pl.BlockSpec((*lead, tm), lambda i: (*(0,) * len(lead), i))

    qk_shape = jax.ShapeDtypeStruct((N_HEADS_B, QK_DIM_B, t), BF16)
    return pl.pallas_call(
        _mix_proj_kernel,
        grid=(t // tm,),
        in_specs=[row(D_MODEL), _const_spec((1, D_MODEL)),
                  _const_spec((SWA_IN, D_MODEL)), _const_spec((MLA_IN, D_MODEL)),
                  _const_spec((Q_LORA_RANK, 1)), _const_spec((N_HEADS_B * QK_DIM_B, Q_LORA_RANK)),
                  _const_spec((KV_LORA_RANK, 1)),
                  _const_spec((N_HEADS_B * (QK_NOPE_DIM + V_DIM_B), KV_LORA_RANK)),
                  _const_spec((1, LANES)), _const_spec((1, LANES)),
                  _const_spec((QK_DIM_B, 1)), _const_spec((QK_DIM_B, 1)),
                  cols(ROPE_HALF), cols(ROPE_HALF)],
        out_specs=[row(WIDTH_A), row(LANES), row(LANES),
                   cols(N_HEADS_B, QK_DIM_B), cols(N_HEADS_B, QK_DIM_B), cols(N_HEADS_B // 2, LANES)],
        out_shape=[jax.ShapeDtypeStruct((t, w), BF16) for w in (WIDTH_A, LANES, LANES)]
        + [qk_shape, qk_shape, jax.ShapeDtypeStruct((N_HEADS_B // 2, LANES, t), BF16)],
        compiler_params=_params(1),
        name="mix_proj",
    )(x, g, wa, wm, cqn, wuq, ckvn, wukv, qag, kag, qbg, kbg, cos_t, sin_t)


SWA_BLOCKS_PER_STEP = 8


def _swa_blocks(step, sink_ref, q_ref, k_ref, v_ref, pq_ref, pk_ref, o_ref):
    blk = WINDOW
    lane = lax.broadcasted_iota(jnp.int32, (1, LANES), 1)
    lo = lane < HALF
    qi = lax.broadcasted_iota(jnp.int32, (blk, 2 * blk), 0) + blk
    ki = lax.broadcasted_iota(jnp.int32, (blk, 2 * blk), 1)
    diff = qi - ki
    band = (diff >= 0) & (diff < WINDOW)
    zero = jnp.zeros((), BF16)

    pending = {}

    def scores(r):
        cur = step * SWA_BLOCKS_PER_STEP + r
        rows = slice(r * blk, (r + 1) * blk)
        if r == 0:
            prev = jnp.maximum(cur - 1, 0)
            p0 = pl.multiple_of(prev * blk, blk)
            c0 = pl.multiple_of(cur * blk, blk)
            kk = jnp.concatenate([k_ref[0, pl.ds(p0, blk), :], k_ref[0, pl.ds(c0, blk), :]], axis=0)
            vv = jnp.concatenate([v_ref[0, pl.ds(p0, blk), :], v_ref[0, pl.ds(c0, blk), :]], axis=0)
            valid = band & ((ki >= blk) | (step > 0))
        else:
            prev = cur - 1
            p0 = pl.multiple_of(prev * blk, blk)
            kk = k_ref[0, pl.ds(p0, 2 * blk), :]
            vv = v_ref[0, pl.ds(p0, 2 * blk), :]
            valid = band
        pk = jnp.concatenate([pk_ref[0, prev], pk_ref[0, cur]], axis=1)
        dist = jnp.where(valid, (pq_ref[0, rows, :] - pk).astype(F32), MASK_DIST)
        s_kv = []
        for kv in range(N_KV_HEADS_A):
            keep = lo if kv == 0 else jnp.logical_not(lo)
            qs = [jnp.where(keep, q_ref[0, rows, j * LANES:(j + 1) * LANES], zero) for j in range(GROUP_A)]
            s_kv.append(_dot_nt(jnp.concatenate(qs, axis=0), kk))
        pending[r] = (s_kv, vv, dist)

    def finish(r):
        rows = slice(r * blk, (r + 1) * blk)
        s_kv, vv, dist = pending.pop(r)
        o_kv, sink_kv = [], []
        for kv in range(N_KV_HEADS_A):
            keep = lo if kv == 0 else jnp.logical_not(lo)
            s_all = s_kv[kv]
            ps, sink_terms = [], []
            for j in range(GROUP_A):
                hd = kv * GROUP_A + j
                slope = 2.0 ** (-8.0 * (hd + 1) / N_HEADS_A)
                sink = sink_ref[hd]
                s = s_all[j * blk:(j + 1) * blk] - slope * dist
                m = jnp.maximum(jnp.max(s, axis=-1, keepdims=True), sink)
                ps.append(jnp.exp(s - m).astype(BF16))
                sink_terms.append(jnp.exp(sink - m))
            v_ext = jnp.where(keep, vv, jnp.ones((), BF16))
            o_kv.append(_dot(jnp.concatenate(ps, axis=0), v_ext))
            sink_kv.append(sink_terms)
        for j in range(GROUP_A):
            o0 = o_kv[0][j * blk:(j + 1) * blk]
            o1 = o_kv[1][j * blk:(j + 1) * blk]
            num = jnp.where(lo, o0, o1)
            den = pltpu.roll(jnp.where(lo, o1, o0), HALF, 1) + jnp.where(lo, sink_kv[0][j], sink_kv[1][j])
            o_ref[0, rows, j * LANES:(j + 1) * LANES] = (num / den).astype(BF16)

    scores(0)
    items = []
    for r in range(SWA_BLOCKS_PER_STEP):
        def item(r=r):
            if r + 1 < SWA_BLOCKS_PER_STEP:
                scores(r + 1)
            finish(r)

        items.append(item)
    return items


def _mla_units(q_ref, k_ref, v_ref, o_ref):
    s_len = q_ref.shape[2]
    tq = ATTN_BLOCK_B
    c = QK_DIM_B ** -0.5 * np.log2(np.e)
    causal = (lax.broadcasted_iota(jnp.int32, (tq, tq), 0)
              >= lax.broadcasted_iota(jnp.int32, (tq, tq), 1))
    ones = jnp.ones((V_DIM_B, s_len), BF16)
    lo = lax.broadcasted_iota(jnp.int32, (1, LANES), 1) < HALF

    v_exts = []
    for pair in range(MLA_PAIRS_PER_STEP):
        v = v_ref[pair]
        v_exts.append((jnp.concatenate([v[0:V_DIM_B], ones], axis=0),
                       jnp.concatenate([ones, v[V_DIM_B:]], axis=0)))

    pending = {}

    def qk(pair, i):
        n = (i + 1) * tq
        scores = []
        for half in range(2):
            hd = 2 * pair + half
            q = q_ref[hd, :, i * tq:n].T
            scores.append(_dot(q, k_ref[hd, :, 0:n]))
        pending[pair, i] = scores

    def finish(pair, i):
        n = (i + 1) * tq
        probs = []
        for s in pending.pop((pair, i)):
            diag = jnp.where(causal, s[:, n - tq:], NEG)
            s = diag if i == 0 else jnp.concatenate([s[:, :n - tq], diag], axis=1)
            m = jnp.max(s, axis=-1, keepdims=True)
            probs.append(jnp.exp2((s - m) * c).astype(BF16))
        o = [_dot_nt(p, v_exts[pair][half][:, 0:n]) for half, p in enumerate(probs)]
        num = jnp.where(lo, o[0], o[1])
        den = pltpu.roll(jnp.where(lo, o[1], o[0]), HALF, 1)
        o_ref[0, i * tq:n, pair * LANES:(pair + 1) * LANES] = (num / den).astype(BF16)

    order = [(pair, i) for i in reversed(range(s_len // tq)) for pair in range(MLA_PAIRS_PER_STEP)]
    qk(*order[0])
    items = []
    for k, (pair, i) in enumerate(order):
        nxt = order[k + 1] if k + 1 < len(order) else None

        def item(pair=pair, i=i, nxt=nxt):
            if nxt is not None:
                qk(*nxt)
            finish(pair, i)

        items.append(item)
    return items


def _attn_kernel(sink_ref, qa_ref, ka_ref, va_ref, pq_ref, pk_ref, qb_ref, kb_ref, vb_ref, oa_ref, ob_ref):
    swa = _swa_blocks(pl.program_id(1), sink_ref, qa_ref, ka_ref, va_ref, pq_ref, pk_ref, oa_ref)
    mla = _mla_units(qb_ref, kb_ref, vb_ref, ob_ref)
    for k in range(max(len(swa), len(mla))):
        for items in (mla, swa):
            if k < len(items):
                items[k]()


def _attention(sinks, qa, ka, va, pos_col, pos_blocks, qb, kb, vb):
    b, s, _ = qa.shape
    rows = SWA_BLOCKS_PER_STEP * WINDOW
    pp = MLA_PAIRS_PER_STEP
    assert s // rows == N_HEADS_B // (2 * pp)
    cur = lambda bi, j: (bi, j, 0)
    whole = lambda bi, j: (bi, 0, 0)
    qk_spec = pl.BlockSpec((2 * pp, QK_DIM_B, s), lambda bi, j: (j, 0, bi))
    out = jax.ShapeDtypeStruct((b, s, WIDTH_A), BF16)
    return pl.pallas_call(
        _attn_kernel,
        grid=(b, s // rows),
        in_specs=[pl.BlockSpec(memory_space=pltpu.SMEM),
                  pl.BlockSpec((1, rows, WIDTH_A), cur),
                  pl.BlockSpec((1, s, LANES), whole), pl.BlockSpec((1, s, LANES), whole),
                  pl.BlockSpec((1, rows, 1), cur),
                  pl.BlockSpec((1, s // WINDOW, 1, WINDOW), lambda bi, j: (bi, 0, 0, 0)),
                  qk_spec, qk_spec, pl.BlockSpec((pp, LANES, s), lambda bi, j: (j, 0, bi))],
        out_specs=[pl.BlockSpec((1, rows, WIDTH_A), cur),
                   pl.BlockSpec((1, s, pp * LANES), lambda bi, j: (bi, 0, j))],
        out_shape=[out, out],
        compiler_params=_params(2),
        name="attention",
    )(sinks, qa, ka, va, pos_col, pos_blocks, qb, kb, vb)


def _merge_kernel(x_ref, g_ref, oa_ref, ob_ref, wga_ref, wgb_ref, wba_ref, wbb_ref, wo_ref, o_ref):
    x = x_ref[...]
    h = _rms(x, g_ref[...], D_MODEL).astype(BF16)
    ga = _dot_nt(h, wga_ref[...])
    gb = _dot_nt(h, wgb_ref[...])
    ma = _dot(oa_ref[...], wba_ref[...])
    mb = _dot(ob_ref[...], wbb_ref[...])
    merged = jax.nn.sigmoid(ga) * ma + jax.nn.sigmoid(gb) * mb
    o_ref[...] = x + _dot(merged.astype(BF16), wo_ref[...])


def _merge(x, g, oa, ob, wga, wgb, wba, wbb, wo):
    t = x.shape[0]
    tm = TOKEN_TILE

    def row(w):
        return pl.BlockSpec((tm, w), lambda i: (i, 0))

    return pl.pallas_call(
        _merge_kernel,
        grid=(t // tm,),
        in_specs=[row(D_MODEL), _const_spec((1, D_MODEL)), row(WIDTH_A), row(WIDTH_B),
                  _const_spec((D_MODEL, D_MODEL)), _const_spec((D_MODEL, D_MODEL)),
                  _const_spec((WIDTH_A, D_MODEL)), _const_spec((WIDTH_B, D_MODEL)),
                  _const_spec((D_MODEL, D_MODEL))],
        out_specs=row(D_MODEL),
        out_shape=jax.ShapeDtypeStruct((t, D_MODEL), F32),
        compiler_params=_params(1),
        name="merge",
    )(x, g, oa, ob, wga, wgb, wba, wbb, wo)


def kernel(x, positions, ffn1_norm, ffn1_w_gate, ffn1_w_up, ffn1_w_down, mix_norm, w_in, swa_q_norm, swa_k_norm, swa_sinks, mla_q_lora_norm, mla_w_uq, mla_kv_lora_norm, mla_w_ukv, mla_q_norm, mla_k_norm, w_branch_a, w_branch_b, w_out, ffn2_norm, ffn2_w_gate, ffn2_w_up, ffn2_w_down):
    b, s, d = x.shape
    t = b * s
    depth = w_in.shape[0]
    assert d == D_MODEL and s % ATTN_BLOCK_B == 0 and t % TOKEN_TILE == 0 and t % min(ROPE_TILE, t) == 0
    assert s % (SWA_BLOCKS_PER_STEP * WINDOW) == 0

    cos_t, sin_t = _rope_tables(positions.reshape(1, t))
    pos_col3 = positions.reshape(b, s, 1)
    pos_blocks = positions.reshape(b, s // WINDOW, 1, WINDOW)

    ffn1_stack = (ffn1_w_gate, ffn1_w_up, ffn1_w_down)
    ffn2_stack = (ffn2_w_gate, ffn2_w_up, ffn2_w_down)
    ffn_w = tuple(w[0].astype(BF16) for w in ffn1_stack)
    w_in_t = jnp.swapaxes(w_in, 1, 2)

    def head_slots(r):
        return jnp.where(r < N_HEADS_A, r // 2 + GROUP_A * (r % 2), r)

    xt = x.reshape(t, d)
    for l in range(depth):
        def mixer_jobs(n_steps, l=l):
            return ([_cast_job(w, l, n_steps) for w in ffn2_stack]
                    + [_cast_job(w_in_t, l, n_steps, OFF_QA, SWA_IN, blk=HEAD_DIM_A, row_block=head_slots),
                       _cast_job(w_in_t, l, n_steps, OFF_CQ, MLA_IN),
                       _cast_job(w_in_t, l, n_steps, OFF_GA, D_MODEL),
                       _cast_job(w_in_t, l, n_steps, OFF_GB, D_MODEL),
                       _cast_job(w_branch_a, l, n_steps, blk=HEAD_DIM_A, row_block=head_slots),
                       _cast_job(w_branch_b, l, n_steps), _cast_job(w_out, l, n_steps)])

        xt, (*ffn_w, wa, wm, wga, wgb, wba, wbb, wo) = _ffn(xt, ffn1_norm[l].reshape(1, d), *ffn_w,
                                                            make_jobs=mixer_jobs)
        qag = (jnp.tile(swa_q_norm[l], 2) * HEAD_DIM_A ** -0.5).reshape(1, LANES)
        kag = jnp.tile(swa_k_norm[l], 2).reshape(1, LANES)
        qa, ka, va, qb, kb, vb = _mix_proj(
            xt, mix_norm[l].reshape(1, d), wa, wm,
            mla_q_lora_norm[l].reshape(-1, 1), mla_w_uq[l].T.astype(BF16),
            mla_kv_lora_norm[l].reshape(-1, 1), mla_w_ukv[l].T.astype(BF16), qag, kag,
            mla_q_norm[l].reshape(-1, 1), mla_k_norm[l].reshape(-1, 1), cos_t, sin_t)

        oa, ob = _attention(swa_sinks[l], qa.reshape(b, s, -1), ka.reshape(b, s, -1), va.reshape(b, s, -1),
                            pos_col3, pos_blocks, qb, kb, vb)

        xt = _merge(xt, mix_norm[l].reshape(1, d), oa.reshape(t, -1), ob.reshape(t, -1),
                    wga, wgb, wba, wbb, wo)

        def next_ffn_jobs(n_steps, l=l):
            return [_cast_job(w, l + 1, n_steps) for w in ffn1_stack] if l + 1 < depth else []

        xt, ffn_w = _ffn(xt, ffn2_norm[l].reshape(1, d), *ffn_w, make_jobs=next_ffn_jobs)
    return xt.reshape(b, s, d)
```

```python
import functools
from typing import NamedTuple

import numpy as np
import jax
import jax.numpy as jnp
from jax import lax
from jax.experimental import pallas as pl
from jax.experimental.pallas import tpu as pltpu

F32 = jnp.float32
BF16 = jnp.bfloat16

D_MODEL = 1024
D_FF = 2816
HEAD_DIM_A = 64
N_HEADS_A = 8
N_KV_HEADS_A = 2
GROUP_A = N_HEADS_A // N_KV_HEADS_A
WINDOW = 128
N_HEADS_B = 8
Q_LORA_RANK = 256
KV_LORA_RANK = 128
QK_NOPE_DIM = 64
QK_ROPE_DIM = 32
QK_DIM_B = QK_NOPE_DIM + QK_ROPE_DIM
V_DIM_B = 64
ROPE_BASE = 10000.0
WIDTH_A = N_HEADS_A * HEAD_DIM_A
WIDTH_B = N_HEADS_B * V_DIM_B
EPS = 1e-6
NEG = -1e30
MASK_DIST = 1e30

LANES = 128
HALF = LANES // 2
BF16_SUBLANES = 16

OFF_QA = 0
OFF_KA = OFF_QA + WIDTH_A
OFF_VA = OFF_KA + N_KV_HEADS_A * HEAD_DIM_A
OFF_CQ = OFF_VA + N_KV_HEADS_A * HEAD_DIM_A
OFF_CKV = OFF_CQ + Q_LORA_RANK
OFF_KR = OFF_CKV + KV_LORA_RANK
OFF_GA = OFF_KR + QK_ROPE_DIM
OFF_GB = OFF_GA + D_MODEL
IN_WIDTH = OFF_GB + D_MODEL

TOKEN_TILE = 1024
FF_CHUNK = 256
MERGE_SUBTILES = 2
PROJ_SUBTILES = 2
ATTN_BLOCK_B = 512
MLA_PAIRS_PER_STEP = 2
ROPE_TILE = 4096
VMEM_LIMIT = 56 * 1024 * 1024

ROPE_HALF = QK_ROPE_DIM // 2
SWA_IN = OFF_CQ
MLA_IN = Q_LORA_RANK + KV_LORA_RANK + QK_ROPE_DIM

_SWA_HEAD_ORDER = np.array([h for j in range(GROUP_A) for h in (j, j + GROUP_A)])


def _rms(x, g, n):
    ssq = jnp.sum(x * x, axis=-1, keepdims=True)
    return x * lax.rsqrt(ssq / n + EPS) * g


def _rms_rows(c, g, n):
    ssq = jnp.sum(c * c, axis=0, keepdims=True)
    return c * lax.rsqrt(ssq / n + EPS) * g


def _dot(a, b):
    return jnp.dot(a, b, preferred_element_type=F32)


def _dot_nt(a, b):
    return lax.dot_general(a, b, (((1,), (1,)), ((), ())), preferred_element_type=F32)


def _const_spec(shape):
    return pl.BlockSpec(shape, lambda *_: (0,) * len(shape), pipeline_mode=pl.Buffered(1))


def _params(n_axes):
    return pltpu.CompilerParams(dimension_semantics=("parallel",) * n_axes,
                                vmem_limit_bytes=VMEM_LIMIT)


class _CastJob(NamedTuple):
    src: jax.Array
    in_spec: pl.BlockSpec
    out_spec: pl.BlockSpec
    out_shape: tuple


def _cast_job(src, layer, n_steps, row0=0, n_rows=None, blk=None, row_block=lambda r: r):
    _, rows, cols = src.shape
    n_rows = rows - row0 if n_rows is None else n_rows
    if blk is None:
        blk = next(r for r in range(BF16_SUBLANES, n_rows + 1, BF16_SUBLANES)
                   if n_rows % r == 0 and row0 % r == 0 and n_rows // r <= n_steps)
    assert n_rows % blk == 0 and row0 % blk == 0 and n_rows // blk <= n_steps
    last = n_rows // blk - 1
    first = row0 // blk
    return _CastJob(src,
                    pl.BlockSpec((None, blk, cols),
                                 lambda i: (layer, first + row_block(jnp.minimum(i, last)), 0)),
                    pl.BlockSpec((blk, cols), lambda i: (jnp.minimum(i, last), 0)),
                    (n_rows, cols))


def _ffn_kernel(*refs, n_jobs):
    x_ref, g_ref, wg_ref, wu_ref, wd_ref = refs[:5]
    o_ref = refs[5 + n_jobs]
    x = x_ref[...]
    h = _rms(x, g_ref[...], D_MODEL).astype(BF16)
    y = None
    for c in range(D_FF // FF_CHUNK):
        sl = slice(c * FF_CHUNK, (c + 1) * FF_CHUNK)
        gate = _dot(h, wg_ref[:, sl])
        up = _dot(h, wu_ref[:, sl])
        act = (gate * jax.nn.sigmoid(gate) * up).astype(BF16)
        d = _dot(act, wd_ref[sl, :])
        y = d if y is None else y + d
    o_ref[...] = x + 0.5 * y
    for src_ref, dst_ref in zip(refs[5:5 + n_jobs], refs[6 + n_jobs:]):
        dst_ref[...] = src_ref[...].astype(BF16)


def _ffn(x, g, wg, wu, wd, make_jobs=lambda n_steps: ()):
    t = x.shape[0]
    tm = TOKEN_TILE
    n_steps = t // tm
    jobs = tuple(make_jobs(n_steps))
    row = pl.BlockSpec((tm, D_MODEL), lambda i: (i, 0))
    out = pl.pallas_call(
        functools.partial(_ffn_kernel, n_jobs=len(jobs)),
        grid=(n_steps,),
        in_specs=[row, _const_spec((1, D_MODEL)), _const_spec((D_MODEL, D_FF)),
                  _const_spec((D_MODEL, D_FF)), _const_spec((D_FF, D_MODEL))] + [j.in_spec for j in jobs],
        out_specs=[row] + [j.out_spec for j in jobs],
        out_shape=[jax.ShapeDtypeStruct((t, D_MODEL), F32)]
        + [jax.ShapeDtypeStruct(j.out_shape, BF16) for j in jobs],
        compiler_params=pltpu.CompilerParams(dimension_semantics=("arbitrary",),
                                             vmem_limit_bytes=VMEM_LIMIT),
        name="ffn",
    )(x, g, wg, wu, wd, *[j.src for j in jobs])
    return out[0], tuple(out[1:])


def _rope_kernel(pos_ref, invf_ref, c_ref, s_ref):
    ang = pos_ref[...].astype(F32) * invf_ref[...]
    c_ref[...] = jnp.cos(ang)
    s_ref[...] = jnp.sin(ang)


def _rope_tables(pos_row):
    t = pos_row.shape[1]
    tm = min(ROPE_TILE, t)
    inv_freq = ROPE_BASE ** (-jnp.arange(ROPE_HALF, dtype=F32) / ROPE_HALF)
    out = jax.ShapeDtypeStruct((ROPE_HALF, t), F32)
    return pl.pallas_call(
        _rope_kernel,
        grid=(t // tm,),
        in_specs=[pl.BlockSpec((1, tm), lambda i: (0, i)), _const_spec((ROPE_HALF, 1))],
        out_specs=[pl.BlockSpec((ROPE_HALF, tm), lambda i: (0, i))] * 2,
        out_shape=[out, out],
        compiler_params=_params(1),
        name="rope_tables",
    )(pos_row, inv_freq.reshape(ROPE_HALF, 1))


def _mix_proj_kernel(x_ref, g_ref, wa_ref, wm_ref, cqn_ref, wuq_ref, ckvn_ref, wukv_ref,
                     qag_ref, kag_ref, qbg_ref, kbg_ref, cos_ref, sin_ref,
                     qa_ref, ka_ref, va_ref, qb_ref, kb_ref, vb_ref):
    lane = lax.broadcasted_iota(jnp.int32, (1, LANES), 1)
    lo = lane < HALF
    per_head = QK_NOPE_DIM + V_DIM_B

    def input_dots(rows):
        h = _rms(x_ref[rows, :], g_ref[...], D_MODEL).astype(BF16)
        return _dot_nt(h, wa_ref[...]), _dot_nt(wm_ref[...], h)

    def half_norm(c, gain):
        sq = c * c
        s_lo = jnp.sum(jnp.where(lo, sq, 0.0), axis=-1, keepdims=True)
        s_hi = jnp.sum(jnp.where(lo, 0.0, sq), axis=-1, keepdims=True)
        inv = jnp.where(lo, lax.rsqrt(s_lo / HEAD_DIM_A + EPS), lax.rsqrt(s_hi / HEAD_DIM_A + EPS))
        return c * inv * gain

    def finish(rows, pa, pm):
        for j in range(WIDTH_A // LANES):
            c = pa[:, OFF_QA + j * LANES: OFF_QA + (j + 1) * LANES]
            qa_ref[rows, j * LANES:(j + 1) * LANES] = half_norm(c, qag_ref[...]).astype(BF16)
        ka_ref[rows, :] = half_norm(pa[:, OFF_KA:OFF_VA], kag_ref[...]).astype(BF16)
        va_ref[rows, :] = pa[:, OFF_VA:OFF_CQ].astype(BF16)

        cq = _rms_rows(pm[0:Q_LORA_RANK], cqn_ref[...], Q_LORA_RANK).astype(BF16)
        ckv = _rms_rows(pm[Q_LORA_RANK:Q_LORA_RANK + KV_LORA_RANK], ckvn_ref[...], KV_LORA_RANK).astype(BF16)
        kr = pm[Q_LORA_RANK + KV_LORA_RANK:MLA_IN]
        qup = _dot(wuq_ref[...], cq)
        kvup = _dot(wukv_ref[...], ckv)
        cos = cos_ref[:, rows]
        sin = sin_ref[:, rows]

        def norm_rope(c, gain):
            y = _rms_rows(c, gain, QK_DIM_B)
            y1 = y[QK_NOPE_DIM:QK_NOPE_DIM + ROPE_HALF]
            y2 = y[QK_NOPE_DIM + ROPE_HALF:QK_DIM_B]
            return jnp.concatenate([y[0:QK_NOPE_DIM], y1 * cos - y2 * sin, y2 * cos + y1 * sin],
                                   axis=0).astype(BF16)

        for hd in range(N_HEADS_B):
            qb_ref[hd, :, rows] = norm_rope(qup[hd * QK_DIM_B:(hd + 1) * QK_DIM_B], qbg_ref[...])
            k_nope = kvup[hd * per_head:hd * per_head + QK_NOPE_DIM]
            kb_ref[hd, :, rows] = norm_rope(jnp.concatenate([k_nope, kr], axis=0), kbg_ref[...])
            v = kvup[hd * per_head + QK_NOPE_DIM:(hd + 1) * per_head]
            vb_ref[hd // 2, (hd % 2) * V_DIM_B:(hd % 2 + 1) * V_DIM_B, rows] = v.astype(BF16)

    sub = x_ref.shape[0] // PROJ_SUBTILES
    groups = [slice(r * sub, (r + 1) * sub) for r in range(PROJ_SUBTILES)]
    nxt = input_dots(groups[0])
    for r, rows in enumerate(groups):
        cur = nxt
        if r + 1 < len(groups):
            nxt = input_dots(groups[r + 1])
        finish(rows, *cur)


def _mix_proj(x, g, wa, wm, cqn, wuq, ckvn, wukv, qag, kag, qbg, kbg, cos_t, sin_t):
    t = x.shape[0]
    tm = TOKEN_TILE

    def row(w):
        return pl.BlockSpec((tm, w), lambda i: (i, 0))

    def cols(*lead):
        return pl.BlockSpec((*lead, tm), lambda i: (*(0,) * len(lead), i))

    qk_shape = jax.ShapeDtypeStruct((N_HEADS_B, QK_DIM_B, t), BF16)
    return pl.pallas_call(
        _mix_proj_kernel,
        grid=(t // tm,),
        in_specs=[row(D_MODEL), _const_spec((1, D_MODEL)),
                  _const_spec((SWA_IN, D_MODEL)), _const_spec((MLA_IN, D_MODEL)),
                  _const_spec((Q_LORA_RANK, 1)), _const_spec((N_HEADS_B * QK_DIM_B, Q_LORA_RANK)),
                  _const_spec((KV_LORA_RANK, 1)),
                  _const_spec((N_HEADS_B * (QK_NOPE_DIM + V_DIM_B), KV_LORA_RANK)),
                  _const_spec((1, LANES)), _const_spec((1, LANES)),
                  _const_spec((QK_DIM_B, 1)), _const_spec((QK_DIM_B, 1)),
                  cols(ROPE_HALF), cols(ROPE_HALF)],
        out_specs=[row(WIDTH_A), row(LANES), row(LANES),
                   cols(N_HEADS_B, QK_DIM_B), cols(N_HEADS_B, QK_DIM_B), cols(N_HEADS_B // 2, LANES)],
        out_shape=[jax.ShapeDtypeStruct((t, w), BF16) for w in (WIDTH_A, LANES, LANES)]
        + [qk_shape, qk_shape, jax.ShapeDtypeStruct((N_HEADS_B // 2, LANES, t), BF16)],
        compiler_params=_params(1),
        name="mix_proj",
    )(x, g, wa, wm, cqn, wuq, ckvn, wukv, qag, kag, qbg, kbg, cos_t, sin_t)


SWA_BLOCKS_PER_STEP = 8


def _swa_blocks(step, sink_ref, q_ref, k_ref, v_ref, pq_ref, pk_ref, o_ref):
    blk = WINDOW
    lane = lax.broadcasted_iota(jnp.int32, (1, LANES), 1)
    lo = lane < HALF
    qi = lax.broadcasted_iota(jnp.int32, (blk, 2 * blk), 0) + blk
    ki = lax.broadcasted_iota(jnp.int32, (blk, 2 * blk), 1)
    diff = qi - ki
    band = (diff >= 0) & (diff < WINDOW)
    zero = jnp.zeros((), BF16)

    pending = {}

    def scores(r):
        cur = step * SWA_BLOCKS_PER_STEP + r
        rows = slice(r * blk, (r + 1) * blk)
        if r == 0:
            prev = jnp.maximum(cur - 1, 0)
            p0 = pl.multiple_of(prev * blk, blk)
            c0 = pl.multiple_of(cur * blk, blk)
            kk = jnp.concatenate([k_ref[0, pl.ds(p0, blk), :], k_ref[0, pl.ds(c0, blk), :]], axis=0)
            vv = jnp.concatenate([v_ref[0, pl.ds(p0, blk), :], v_ref[0, pl.ds(c0, blk), :]], axis=0)
            valid = band & ((ki >= blk) | (step > 0))
        else:
            prev = cur - 1
            p0 = pl.multiple_of(prev * blk, blk)
            kk = k_ref[0, pl.ds(p0, 2 * blk), :]
            vv = v_ref[0, pl.ds(p0, 2 * blk), :]
            valid = band
        pk = jnp.concatenate([pk_ref[0, prev], pk_ref[0, cur]], axis=1)
        dist = jnp.where(valid, (pq_ref[0, rows, :] - pk).astype(F32), MASK_DIST)
        s_kv = []
        for kv in range(N_KV_HEADS_A):
            keep = lo if kv == 0 else jnp.logical_not(lo)
            qs = [jnp.where(keep, q_ref[0, rows, j * LANES:(j + 1) * LANES], zero) for j in range(GROUP_A)]
            s_kv.append(_dot_nt(jnp.concatenate(qs, axis=0), kk))
        pending[r] = (s_kv, vv, dist)

    def finish(r):
        rows = slice(r * blk, (r + 1) * blk)
        s_kv, vv, dist = pending.pop(r)
        o_kv, sink_kv = [], []
        for kv in range(N_KV_HEADS_A):
            keep = lo if kv == 0 else jnp.logical_not(lo)
            s_all = s_kv[kv]
            ps, sink_terms = [], []
            for j in range(GROUP_A):
                hd = kv * GROUP_A + j
                slope = 2.0 ** (-8.0 * (hd + 1) / N_HEADS_A)
                sink = sink_ref[hd]
                s = s_all[j * blk:(j + 1) * blk] - slope * dist
                m = jnp.maximum(jnp.max(s, axis=-1, keepdims=True), sink)
                ps.append(jnp.exp(s - m).astype(BF16))
                sink_terms.append(jnp.exp(sink - m))
            v_ext = jnp.where(keep, vv, jnp.ones((), BF16))
            o_kv.append(_dot(jnp.concatenate(ps, axis=0), v_ext))
            sink_kv.append(sink_terms)
        for j in range(GROUP_A):
            o0 = o_kv[0][j * blk:(j + 1) * blk]
            o1 = o_kv[1][j * blk:(j + 1) * blk]
            num = jnp.where(lo, o0, o1)
            den = pltpu.roll(jnp.where(lo, o1, o0), HALF, 1) + jnp.where(lo, sink_kv[0][j], sink_kv[1][j])
            o_ref[0, rows, j * LANES:(j + 1) * LANES] = (num / den).astype(BF16)

    scores(0)
    items = []
    for r in range(SWA_BLOCKS_PER_STEP):
        def item(r=r):
            if r + 1 < SWA_BLOCKS_PER_STEP:
                scores(r + 1)
            finish(r)

        items.append(item)
    return items


def _mla_units(q_ref, k_ref, v_ref, o_ref):
    s_len = q_ref.shape[2]
    tq = ATTN_BLOCK_B
    c = QK_DIM_B ** -0.5 * np.log2(np.e)
    causal = (lax.broadcasted_iota(jnp.int32, (tq, tq), 0)
              >= lax.broadcasted_iota(jnp.int32, (tq, tq), 1))
    ones = jnp.ones((V_DIM_B, s_len), BF16)
    lo = lax.broadcasted_iota(jnp.int32, (1, LANES), 1) < HALF

    v_exts = []
    for pair in range(MLA_PAIRS_PER_STEP):
        v = v_ref[pair]
        v_exts.append((jnp.concatenate([v[0:V_DIM_B], ones], axis=0),
                       jnp.concatenate([ones, v[V_DIM_B:]], axis=0)))

    pending = {}

    def qk(pair, i):
        n = (i + 1) * tq
        scores = []
        for half in range(2):
            hd = 2 * pair + half
            q = q_ref[hd, :, i * tq:n].T
            scores.append(_dot(q, k_ref[hd, :, 0:n]))
        pending[pair, i] = scores

    def finish(pair, i):
        n = (i + 1) * tq
        probs = []
        for s in pending.pop((pair, i)):
            diag = jnp.where(causal, s[:, n - tq:], NEG)
            s = diag if i == 0 else jnp.concatenate([s[:, :n - tq], diag], axis=1)
            m = jnp.max(s, axis=-1, keepdims=True)
            probs.append(jnp.exp2((s - m) * c).astype(BF16))
        o = [_dot_nt(p, v_exts[pair][half][:, 0:n]) for half, p in enumerate(probs)]
        num = jnp.where(lo, o[0], o[1])
        den = pltpu.roll(jnp.where(lo, o[1], o[0]), HALF, 1)
        o_ref[0, i * tq:n, pair * LANES:(pair + 1) * LANES] = (num / den).astype(BF16)

    order = [(pair, i) for i in reversed(range(s_len // tq)) for pair in range(MLA_PAIRS_PER_STEP)]
    qk(*order[0])
    items = []
    for k, (pair, i) in enumerate(order):
        nxt = order[k + 1] if k + 1 < len(order) else None

        def item(pair=pair, i=i, nxt=nxt):
            if nxt is not None:
                qk(*nxt)
            finish(pair, i)

        items.append(item)
    return items


def _attn_kernel(sink_ref, qa_ref, ka_ref, va_ref, pq_ref, pk_ref, qb_ref, kb_ref, vb_ref, oa_ref, ob_ref):
    swa = _swa_blocks(pl.program_id(1), sink_ref, qa_ref, ka_ref, va_ref, pq_ref, pk_ref, oa_ref)
    mla = _mla_units(qb_ref, kb_ref, vb_ref, ob_ref)
    for k in range(max(len(swa), len(mla))):
        for items in (mla, swa):
            if k < len(items):
                items[k]()


def _attention(sinks, qa, ka, va, pos_col, pos_blocks, qb, kb, vb):
    b, s, _ = qa.shape
    rows = SWA_BLOCKS_PER_STEP * WINDOW
    pp = MLA_PAIRS_PER_STEP
    assert s // rows == N_HEADS_B // (2 * pp)
    cur = lambda bi, j: (bi, j, 0)
    whole = lambda bi, j: (bi, 0, 0)
    qk_spec = pl.BlockSpec((2 * pp, QK_DIM_B, s), lambda bi, j: (j, 0, bi))
    out = jax.ShapeDtypeStruct((b, s, WIDTH_A), BF16)
    return pl.pallas_call(
        _attn_kernel,
        grid=(b, s // rows),
        in_specs=[pl.BlockSpec(memory_space=pltpu.SMEM),
                  pl.BlockSpec((1, rows, WIDTH_A), cur),
                  pl.BlockSpec((1, s, LANES), whole), pl.BlockSpec((1, s, LANES), whole),
                  pl.BlockSpec((1, rows, 1), cur),
                  pl.BlockSpec((1, s // WINDOW, 1, WINDOW), lambda bi, j: (bi, 0, 0, 0)),
                  qk_spec, qk_spec, pl.BlockSpec((pp, LANES, s), lambda bi, j: (j, 0, bi))],
        out_specs=[pl.BlockSpec((1, rows, WIDTH_A), cur),
                   pl.BlockSpec((1, s, pp * LANES), lambda bi, j: (bi, 0, j))],
        out_shape=[out, out],
        compiler_params=_params(2),
        name="attention",
    )(sinks, qa, ka, va, pos_col, pos_blocks, qb, kb, vb)


def _merge_kernel(x_ref, g_ref, oa_ref, ob_ref, wga_ref, wgb_ref, wba_ref, wbb_ref, wo_ref, o_ref):
    def branch_dots(rows):
        x = x_ref[rows, :]
        h = _rms(x, g_ref[...], D_MODEL).astype(BF16)
        return (x, _dot_nt(h, wga_ref[...]), _dot_nt(h, wgb_ref[...]),
                _dot(oa_ref[rows, :], wba_ref[...]), _dot(ob_ref[rows, :], wbb_ref[...]))

    def finish(rows, x, ga, gb, ma, mb):
        merged = jax.nn.sigmoid(ga) * ma + jax.nn.sigmoid(gb) * mb
        o_ref[rows, :] = x + _dot(merged.astype(BF16), wo_ref[...])

    sub = x_ref.shape[0] // MERGE_SUBTILES
    groups = [slice(r * sub, (r + 1) * sub) for r in range(MERGE_SUBTILES)]
    nxt = branch_dots(groups[0])
    for r, rows in enumerate(groups):
        cur = nxt
        if r + 1 < len(groups):
            nxt = branch_dots(groups[r + 1])
        finish(rows, *cur)


def _merge(x, g, oa, ob, wga, wgb, wba, wbb, wo):
    t = x.shape[0]
    tm = TOKEN_TILE

    def row(w):
        return pl.BlockSpec((tm, w), lambda i: (i, 0))

    return pl.pallas_call(
        _merge_kernel,
        grid=(t // tm,),
        in_specs=[row(D_MODEL), _const_spec((1, D_MODEL)), row(WIDTH_A), row(WIDTH_B),
                  _const_spec((D_MODEL, D_MODEL)), _const_spec((D_MODEL, D_MODEL)),
                  _const_spec((WIDTH_A, D_MODEL)), _const_spec((WIDTH_B, D_MODEL)),
                  _const_spec((D_MODEL, D_MODEL))],
        out_specs=row(D_MODEL),
        out_shape=jax.ShapeDtypeStruct((t, D_MODEL), F32),
        compiler_params=_params(1),
        name="merge",
    )(x, g, oa, ob, wga, wgb, wba, wbb, wo)


def kernel(x, positions, ffn1_norm, ffn1_w_gate, ffn1_w_up, ffn1_w_down, mix_norm, w_in, swa_q_norm, swa_k_norm, swa_sinks, mla_q_lora_norm, mla_w_uq, mla_kv_lora_norm, mla_w_ukv, mla_q_norm, mla_k_norm, w_branch_a, w_branch_b, w_out, ffn2_norm, ffn2_w_gate, ffn2_w_up, ffn2_w_down):
    b, s, d = x.shape
    t = b * s
    depth = w_in.shape[0]
    assert d == D_MODEL and s % ATTN_BLOCK_B == 0 and t % TOKEN_TILE == 0 and t % min(ROPE_TILE, t) == 0
    assert s % (SWA_BLOCKS_PER_STEP * WINDOW) == 0

    cos_t, sin_t = _rope_tables(positions.reshape(1, t))
    pos_col3 = positions.reshape(b, s, 1)
    pos_blocks = positions.reshape(b, s // WINDOW, 1, WINDOW)

    ffn1_stack = (ffn1_w_gate, ffn1_w_up, ffn1_w_down)
    ffn2_stack = (ffn2_w_gate, ffn2_w_up, ffn2_w_down)
    ffn_w = tuple(w[0].astype(BF16) for w in ffn1_stack)
    w_in_t = jnp.swapaxes(w_in, 1, 2)

    def head_slots(r):
        return jnp.where(r < N_HEADS_A, r // 2 + GROUP_A * (r % 2), r)

    xt = x.reshape(t, d)
    for l in range(depth):
        def mixer_jobs(n_steps, l=l):
            return ([_cast_job(w, l, n_steps) for w in ffn2_stack]
                    + [_cast_job(w_in_t, l, n_steps, OFF_QA, SWA_IN, blk=HEAD_DIM_A, row_block=head_slots),
                       _cast_job(w_in_t, l, n_steps, OFF_CQ, MLA_IN),
                       _cast_job(w_in_t, l, n_steps, OFF_GA, D_MODEL),
                       _cast_job(w_in_t, l, n_steps, OFF_GB, D_MODEL),
                       _cast_job(w_branch_a, l, n_steps, blk=HEAD_DIM_A, row_block=head_slots),
                       _cast_job(w_branch_b, l, n_steps), _cast_job(w_out, l, n_steps)])

        xt, (*ffn_w, wa, wm, wga, wgb, wba, wbb, wo) = _ffn(xt, ffn1_norm[l].reshape(1, d), *ffn_w,
                                                            make_jobs=mixer_jobs)
        qag = (jnp.tile(swa_q_norm[l], 2) * HEAD_DIM_A ** -0.5).reshape(1, LANES)
        kag = jnp.tile(swa_k_norm[l], 2).reshape(1, LANES)
        qa, ka, va, qb, kb, vb = _mix_proj(
            xt, mix_norm[l].reshape(1, d), wa, wm,
            mla_q_lora_norm[l].reshape(-1, 1), mla_w_uq[l].T.astype(BF16),
            mla_kv_lora_norm[l].reshape(-1, 1), mla_w_ukv[l].T.astype(BF16), qag, kag,
            mla_q_norm[l].reshape(-1, 1), mla_k_norm[l].reshape(-1, 1), cos_t, sin_t)

        oa, ob = _attention(swa_sinks[l], qa.reshape(b, s, -1), ka.reshape(b, s, -1), va.reshape(b, s, -1),
                            pos_col3, pos_blocks, qb, kb, vb)

        xt = _merge(xt, mix_norm[l].reshape(1, d), oa.reshape(t, -1), ob.reshape(t, -1),
                    wga, wgb, wba, wbb, wo)

        def next_ffn_jobs(n_steps, l=l):
            return [_cast_job(w, l + 1, n_steps) for w in ffn1_stack] if l + 1 < depth else []

        xt, ffn_w = _ffn(xt, ffn2_norm[l].reshape(1, d), *ffn_w, make_jobs=next_ffn_jobs)
    return xt.reshape(b, s, d)
```

```python
import functools
from typing import NamedTuple

import numpy as np
import jax
import jax.numpy as jnp
from jax import lax
from jax.experimental import pallas as pl
from jax.experimental.pallas import tpu as pltpu

F32 = jnp.float32
BF16 = jnp.bfloat16

D_MODEL = 1024
D_FF = 2816
HEAD_DIM_A = 64
N_HEADS_A = 8
N_KV_HEADS_A = 2
GROUP_A = N_HEADS_A // N_KV_HEADS_A
WINDOW = 128
N_HEADS_B = 8
Q_LORA_RANK = 256
KV_LORA_RANK = 128
QK_NOPE_DIM = 64
QK_ROPE_DIM = 32
QK_DIM_B = QK_NOPE_DIM + QK_ROPE_DIM
V_DIM_B = 64
ROPE_BASE = 10000.0
WIDTH_A = N_HEADS_A * HEAD_DIM_A
WIDTH_B = N_HEADS_B * V_DIM_B
EPS = 1e-6
NEG = -1e30
MASK_DIST = 1e30

LANES = 128
HALF = LANES // 2
BF16_SUBLANES = 16

OFF_QA = 0
OFF_KA = OFF_QA + WIDTH_A
OFF_VA = OFF_KA + N_KV_HEADS_A * HEAD_DIM_A
OFF_CQ = OFF_VA + N_KV_HEADS_A * HEAD_DIM_A
OFF_CKV = OFF_CQ + Q_LORA_RANK
OFF_KR = OFF_CKV + KV_LORA_RANK
OFF_GA = OFF_KR + QK_ROPE_DIM
OFF_GB = OFF_GA + D_MODEL
IN_WIDTH = OFF_GB + D_MODEL

TOKEN_TILE = 1024
FF_CHUNK = 256
MERGE_SUBTILES = 2
ATTN_BLOCK_B = 512
MLA_PAIRS_PER_STEP = 2
ROPE_TILE = 4096
VMEM_LIMIT = 56 * 1024 * 1024

ROPE_HALF = QK_ROPE_DIM // 2
SWA_IN = OFF_CQ
MLA_IN = Q_LORA_RANK + KV_LORA_RANK + QK_ROPE_DIM

_SWA_HEAD_ORDER = np.array([h for j in range(GROUP_A) for h in (j, j + GROUP_A)])


def _rms(x, g, n):
    ssq = jnp.sum(x * x, axis=-1, keepdims=True)
    return x * lax.rsqrt(ssq / n + EPS) * g


def _rms_rows(c, g, n):
    ssq = jnp.sum(c * c, axis=0, keepdims=True)
    return c * lax.rsqrt(ssq / n + EPS) * g


def _dot(a, b):
    return jnp.dot(a, b, preferred_element_type=F32)


def _dot_nt(a, b):
    return lax.dot_general(a, b, (((1,), (1,)), ((), ())), preferred_element_type=F32)


def _const_spec(shape):
    return pl.BlockSpec(shape, lambda *_: (0,) * len(shape), pipeline_mode=pl.Buffered(1))


def _params(n_axes):
    return pltpu.CompilerParams(dimension_semantics=("parallel",) * n_axes,
                                vmem_limit_bytes=VMEM_LIMIT)


class _CastJob(NamedTuple):
    src: jax.Array
    in_spec: pl.BlockSpec
    out_spec: pl.BlockSpec
    out_shape: tuple


def _cast_job(src, layer, n_steps, row0=0, n_rows=None, blk=None, row_block=lambda r: r):
    _, rows, cols = src.shape
    n_rows = rows - row0 if n_rows is None else n_rows
    if blk is None:
        blk = next(r for r in range(BF16_SUBLANES, n_rows + 1, BF16_SUBLANES)
                   if n_rows % r == 0 and row0 % r == 0 and n_rows // r <= n_steps)
    assert n_rows % blk == 0 and row0 % blk == 0 and n_rows // blk <= n_steps
    last = n_rows // blk - 1
    first = row0 // blk
    return _CastJob(src,
                    pl.BlockSpec((None, blk, cols),
                                 lambda i: (layer, first + row_block(jnp.minimum(i, last)), 0)),
                    pl.BlockSpec((blk, cols), lambda i: (jnp.minimum(i, last), 0)),
                    (n_rows, cols))


def _ffn_kernel(*refs, n_jobs):
    x_ref, g_ref, wg_ref, wu_ref, wd_ref = refs[:5]
    o_ref = refs[5 + n_jobs]
    x = x_ref[...]
    h = _rms(x, g_ref[...], D_MODEL).astype(BF16)
    y = None
    for c in range(D_FF // FF_CHUNK):
        sl = slice(c * FF_CHUNK, (c + 1) * FF_CHUNK)
        gate = _dot(h, wg_ref[:, sl])
        up = _dot(h, wu_ref[:, sl])
        act = (gate * jax.nn.sigmoid(gate) * up).astype(BF16)
        d = _dot(act, wd_ref[sl, :])
        y = d if y is None else y + d
    o_ref[...] = x + 0.5 * y
    for src_ref, dst_ref in zip(refs[5:5 + n_jobs], refs[6 + n_jobs:]):
        dst_ref[...] = src_ref[...].astype(BF16)


def _ffn(x, g, wg, wu, wd, make_jobs=lambda n_steps: ()):
    t = x.shape[0]
    tm = TOKEN_TILE
    n_steps = t // tm
    jobs = tuple(make_jobs(n_steps))
    row = pl.BlockSpec((tm, D_MODEL), lambda i: (i, 0))
    out = pl.pallas_call(
        functools.partial(_ffn_kernel, n_jobs=len(jobs)),
        grid=(n_steps,),
        in_specs=[row, _const_spec((1, D_MODEL)), _const_spec((D_MODEL, D_FF)),
                  _const_spec((D_MODEL, D_FF)), _const_spec((D_FF, D_MODEL))] + [j.in_spec for j in jobs],
        out_specs=[row] + [j.out_spec for j in jobs],
        out_shape=[jax.ShapeDtypeStruct((t, D_MODEL), F32)]
        + [jax.ShapeDtypeStruct(j.out_shape, BF16) for j in jobs],
        compiler_params=pltpu.CompilerParams(dimension_semantics=("arbitrary",),
                                             vmem_limit_bytes=VMEM_LIMIT),
        name="ffn",
    )(x, g, wg, wu, wd, *[j.src for j in jobs])
    return out[0], tuple(out[1:])


def _rope_kernel(pos_ref, invf_ref, c_ref, s_ref):
    ang = pos_ref[...].astype(F32) * invf_ref[...]
    c_ref[...] = jnp.cos(ang)
    s_ref[...] = jnp.sin(ang)


def _rope_tables(pos_row):
    t = pos_row.shape[1]
    tm = min(ROPE_TILE, t)
    inv_freq = ROPE_BASE ** (-jnp.arange(ROPE_HALF, dtype=F32) / ROPE_HALF)
    out = jax.ShapeDtypeStruct((ROPE_HALF, t), F32)
    return pl.pallas_call(
        _rope_kernel,
        grid=(t // tm,),
        in_specs=[pl.BlockSpec((1, tm), lambda i: (0, i)), _const_spec((ROPE_HALF, 1))],
        out_specs=[pl.BlockSpec((ROPE_HALF, tm), lambda i: (0, i))] * 2,
        out_shape=[out, out],
        compiler_params=_params(1),
        name="rope_tables",
    )(pos_row, inv_freq.reshape(ROPE_HALF, 1))


def _mix_proj_kernel(x_ref, g_ref, wa_ref, wm_ref, cqn_ref, wuq_ref, ckvn_ref, wukv_ref,
                     qag_ref, kag_ref, qbg_ref, kbg_ref, cos_ref, sin_ref,
                     qa_ref, ka_ref, va_ref, qb_ref, kb_ref, vb_ref):
    x = x_ref[...]
    h = _rms(x, g_ref[...], D_MODEL).astype(BF16)

    pa = _dot_nt(h, wa_ref[...])
    lane = lax.broadcasted_iota(jnp.int32, (1, LANES), 1)
    lo = lane < HALF

    def half_norm(c, gain):
        sq = c * c
        s_lo = jnp.sum(jnp.where(lo, sq, 0.0), axis=-1, keepdims=True)
        s_hi = jnp.sum(jnp.where(lo, 0.0, sq), axis=-1, keepdims=True)
        inv = jnp.where(lo, lax.rsqrt(s_lo / HEAD_DIM_A + EPS), lax.rsqrt(s_hi / HEAD_DIM_A + EPS))
        return c * inv * gain

    for j in range(WIDTH_A // LANES):
        c = pa[:, OFF_QA + j * LANES: OFF_QA + (j + 1) * LANES]
        qa_ref[:, j * LANES:(j + 1) * LANES] = half_norm(c, qag_ref[...]).astype(BF16)
    ka_ref[...] = half_norm(pa[:, OFF_KA:OFF_VA], kag_ref[...]).astype(BF16)
    va_ref[...] = pa[:, OFF_VA:OFF_CQ].astype(BF16)

    pm = _dot_nt(wm_ref[...], h)
    cq = _rms_rows(pm[0:Q_LORA_RANK], cqn_ref[...], Q_LORA_RANK).astype(BF16)
    ckv = _rms_rows(pm[Q_LORA_RANK:Q_LORA_RANK + KV_LORA_RANK], ckvn_ref[...], KV_LORA_RANK).astype(BF16)
    kr = pm[Q_LORA_RANK + KV_LORA_RANK:MLA_IN]
    qup = _dot(wuq_ref[...], cq)
    kvup = _dot(wukv_ref[...], ckv)
    cos = cos_ref[...]
    sin = sin_ref[...]

    def norm_rope(c, gain):
        y = _rms_rows(c, gain, QK_DIM_B)
        y1 = y[QK_NOPE_DIM:QK_NOPE_DIM + ROPE_HALF]
        y2 = y[QK_NOPE_DIM + ROPE_HALF:QK_DIM_B]
        return jnp.concatenate([y[0:QK_NOPE_DIM], y1 * cos - y2 * sin, y2 * cos + y1 * sin],
                               axis=0).astype(BF16)

    per_head = QK_NOPE_DIM + V_DIM_B
    for hd in range(N_HEADS_B):
        qb_ref[hd] = norm_rope(qup[hd * QK_DIM_B:(hd + 1) * QK_DIM_B], qbg_ref[...])
        k_nope = kvup[hd * per_head:hd * per_head + QK_NOPE_DIM]
        kb_ref[hd] = norm_rope(jnp.concatenate([k_nope, kr], axis=0), kbg_ref[...])
        v = kvup[hd * per_head + QK_NOPE_DIM:(hd + 1) * per_head]
        vb_ref[hd // 2, (hd % 2) * V_DIM_B:(hd % 2 + 1) * V_DIM_B, :] = v.astype(BF16)


def _mix_proj(x, g, wa, wm, cqn, wuq, ckvn, wukv, qag, kag, qbg, kbg, cos_t, sin_t):
    t = x.shape[0]
    tm = TOKEN_TILE

    def row(w):
        return pl.BlockSpec((tm, w), lambda i: (i, 0))

    def cols(*lead):
        return pl.BlockSpec((*lead, tm), lambda i: (*(0,) * len(lead), i))

    qk_shape = jax.ShapeDtypeStruct((N_HEADS_B, QK_DIM_B, t), BF16)
    return pl.pallas_call(
        _mix_proj_kernel,
        grid=(t // tm,),
        in_specs=[row(D_MODEL), _const_spec((1, D_MODEL)),
                  _const_spec((SWA_IN, D_MODEL)), _const_spec((MLA_IN, D_MODEL)),
                  _const_spec((Q_LORA_RANK, 1)), _const_spec((N_HEADS_B * QK_DIM_B, Q_LORA_RANK)),
                  _const_spec((KV_LORA_RANK, 1)),
                  _const_spec((N_HEADS_B * (QK_NOPE_DIM + V_DIM_B), KV_LORA_RANK)),
                  _const_spec((1, LANES)), _const_spec((1, LANES)),
                  _const_spec((QK_DIM_B, 1)), _const_spec((QK_DIM_B, 1)),
                  cols(ROPE_HALF), cols(ROPE_HALF)],
        out_specs=[row(WIDTH_A), row(LANES), row(LANES),
                   cols(N_HEADS_B, QK_DIM_B), cols(N_HEADS_B, QK_DIM_B), cols(N_HEADS_B // 2, LANES)],
        out_shape=[jax.ShapeDtypeStruct((t, w), BF16) for w in (WIDTH_A, LANES, LANES)]
        + [qk_shape, qk_shape, jax.ShapeDtypeStruct((N_HEADS_B // 2, LANES, t), BF16)],
        compiler_params=_params(1),
        name="mix_proj",
    )(x, g, wa, wm, cqn, wuq, ckvn, wukv, qag, kag, qbg, kbg, cos_t, sin_t)


SWA_BLOCKS_PER_STEP = 8


def _swa_blocks(step, sink_ref, q_ref, k_ref, v_ref, pq_ref, pk_ref, o_ref):
    blk = WINDOW
    lane = lax.broadcasted_iota(jnp.int32, (1, LANES), 1)
    lo = lane < HALF
    qi = lax.broadcasted_iota(jnp.int32, (blk, 2 * blk), 0) + blk
    ki = lax.broadcasted_iota(jnp.int32, (blk, 2 * blk), 1)
    diff = qi - ki
    band = (diff >= 0) & (diff < WINDOW)
    zero = jnp.zeros((), BF16)

    pending = {}

    def scores(r):
        cur = step * SWA_BLOCKS_PER_STEP + r
        rows = slice(r * blk, (r + 1) * blk)
        if r == 0:
            prev = jnp.maximum(cur - 1, 0)
            p0 = pl.multiple_of(prev * blk, blk)
            c0 = pl.multiple_of(cur * blk, blk)
            kk = jnp.concatenate([k_ref[0, pl.ds(p0, blk), :], k_ref[0, pl.ds(c0, blk), :]], axis=0)
            vv = jnp.concatenate([v_ref[0, pl.ds(p0, blk), :], v_ref[0, pl.ds(c0, blk), :]], axis=0)
            valid = band & ((ki >= blk) | (step > 0))
        else:
            prev = cur - 1
            p0 = pl.multiple_of(prev * blk, blk)
            kk = k_ref[0, pl.ds(p0, 2 * blk), :]
            vv = v_ref[0, pl.ds(p0, 2 * blk), :]
            valid = band
        pk = jnp.concatenate([pk_ref[0, prev], pk_ref[0, cur]], axis=1)
        dist = jnp.where(valid, (pq_ref[0, rows, :] - pk).astype(F32), MASK_DIST)
        s_kv = []
        for kv in range(N_KV_HEADS_A):
            keep = lo if kv == 0 else jnp.logical_not(lo)
            qs = [jnp.where(keep, q_ref[0, rows, j * LANES:(j + 1) * LANES], zero) for j in range(GROUP_A)]
            s_kv.append(_dot_nt(jnp.concatenate(qs, axis=0), kk))
        pending[r] = (s_kv, vv, dist)

    def finish(r):
        rows = slice(r * blk, (r + 1) * blk)
        s_kv, vv, dist = pending.pop(r)
        o_kv, sink_kv = [], []
        for kv in range(N_KV_HEADS_A):
            keep = lo if kv == 0 else jnp.logical_not(lo)
            s_all = s_kv[kv]
            ps, sink_terms = [], []
            for j in range(GROUP_A):
                hd = kv * GROUP_A + j
                slope = 2.0 ** (-8.0 * (hd + 1) / N_HEADS_A)
                sink = sink_ref[hd]
                s = s_all[j * blk:(j + 1) * blk] - slope * dist
                m = jnp.maximum(jnp.max(s, axis=-1, keepdims=True), sink)
                ps.append(jnp.exp(s - m).astype(BF16))
                sink_terms.append(jnp.exp(sink - m))
            v_ext = jnp.where(keep, vv, jnp.ones((), BF16))
            o_kv.append(_dot(jnp.concatenate(ps, axis=0), v_ext))
            sink_kv.append(sink_terms)
        for j in range(GROUP_A):
            o0 = o_kv[0][j * blk:(j + 1) * blk]
            o1 = o_kv[1][j * blk:(j + 1) * blk]
            num = jnp.where(lo, o0, o1)
            den = pltpu.roll(jnp.where(lo, o1, o0), HALF, 1) + jnp.where(lo, sink_kv[0][j], sink_kv[1][j])
            o_ref[0, rows, j * LANES:(j + 1) * LANES] = (num / den).astype(BF16)

    scores(0)
    items = []
    for r in range(SWA_BLOCKS_PER_STEP):
        def item(r=r):
            if r + 1 < SWA_BLOCKS_PER_STEP:
                scores(r + 1)
            finish(r)

        items.append(item)
    return items


def _mla_units(q_ref, k_ref, v_ref, o_ref):
    s_len = q_ref.shape[2]
    tq = ATTN_BLOCK_B
    c = QK_DIM_B ** -0.5 * np.log2(np.e)
    causal = (lax.broadcasted_iota(jnp.int32, (tq, tq), 0)
              >= lax.broadcasted_iota(jnp.int32, (tq, tq), 1))
    ones = jnp.ones((V_DIM_B, s_len), BF16)
    lo = lax.broadcasted_iota(jnp.int32, (1, LANES), 1) < HALF

    v_exts = []
    for pair in range(MLA_PAIRS_PER_STEP):
        v = v_ref[pair]
        v_exts.append((jnp.concatenate([v[0:V_DIM_B], ones], axis=0),
                       jnp.concatenate([ones, v[V_DIM_B:]], axis=0)))

    pending = {}

    def qk(pair, i):
        n = (i + 1) * tq
        scores = []
        for half in range(2):
            hd = 2 * pair + half
            q = q_ref[hd, :, i * tq:n].T
            scores.append(_dot(q, k_ref[hd, :, 0:n]))
        pending[pair, i] = scores

    def finish(pair, i):
        n = (i + 1) * tq
        probs = []
        for s in pending.pop((pair, i)):
            diag = jnp.where(causal, s[:, n - tq:], NEG)
            s = diag if i == 0 else jnp.concatenate([s[:, :n - tq], diag], axis=1)
            m = jnp.max(s, axis=-1, keepdims=True)
            probs.append(jnp.exp2((s - m) * c).astype(BF16))
        o = [_dot_nt(p, v_exts[pair][half][:, 0:n]) for half, p in enumerate(probs)]
        num = jnp.where(lo, o[0], o[1])
        den = pltpu.roll(jnp.where(lo, o[1], o[0]), HALF, 1)
        o_ref[0, i * tq:n, pair * LANES:(pair + 1) * LANES] = (num / den).astype(BF16)

    order = [(pair, i) for i in reversed(range(s_len // tq)) for pair in range(MLA_PAIRS_PER_STEP)]
    qk(*order[0])
    items = []
    for k, (pair, i) in enumerate(order):
        nxt = order[k + 1] if k + 1 < len(order) else None

        def item(pair=pair, i=i, nxt=nxt):
            if nxt is not None:
                qk(*nxt)
            finish(pair, i)

        items.append(item)
    return items


def _attn_kernel(sink_ref, qa_ref, ka_ref, va_ref, pq_ref, pk_ref, qb_ref, kb_ref, vb_ref, oa_ref, ob_ref):
    swa = _swa_blocks(pl.program_id(1), sink_ref, qa_ref, ka_ref, va_ref, pq_ref, pk_ref, oa_ref)
    mla = _mla_units(qb_ref, kb_ref, vb_ref, ob_ref)
    for k in range(max(len(swa), len(mla))):
        for items in (mla, swa):
            if k < len(items):
                items[k]()


def _attention(sinks, qa, ka, va, pos_col, pos_blocks, qb, kb, vb):
    b, s, _ = qa.shape
    rows = SWA_BLOCKS_PER_STEP * WINDOW
    pp = MLA_PAIRS_PER_STEP
    assert s // rows == N_HEADS_B // (2 * pp)
    cur = lambda bi, j: (bi, j, 0)
    whole = lambda bi, j: (bi, 0, 0)
    qk_spec = pl.BlockSpec((2 * pp, QK_DIM_B, s), lambda bi, j: (j, 0, bi))
    out = jax.ShapeDtypeStruct((b, s, WIDTH_A), BF16)
    return pl.pallas_call(
        _attn_kernel,
        grid=(b, s // rows),
        in_specs=[pl.BlockSpec(memory_space=pltpu.SMEM),
                  pl.BlockSpec((1, rows, WIDTH_A), cur),
                  pl.BlockSpec((1, s, LANES), whole), pl.BlockSpec((1, s, LANES), whole),
                  pl.BlockSpec((1, rows, 1), cur),
                  pl.BlockSpec((1, s // WINDOW, 1, WINDOW), lambda bi, j: (bi, 0, 0, 0)),
                  qk_spec, qk_spec, pl.BlockSpec((pp, LANES, s), lambda bi, j: (j, 0, bi))],
        out_specs=[pl.BlockSpec((1, rows, WIDTH_A), cur),
                   pl.BlockSpec((1, s, pp * LANES), lambda bi, j: (bi, 0, j))],
        out_shape=[out, out],
        compiler_params=_params(2),
        name="attention",
    )(sinks, qa, ka, va, pos_col, pos_blocks, qb, kb, vb)


def _merge_kernel(x_ref, g_ref, oa_ref, ob_ref, wga_ref, wgb_ref, wba_ref, wbb_ref, wo_ref, o_ref):
    def branch_dots(rows):
        x = x_ref[rows, :]
        h = _rms(x, g_ref[...], D_MODEL).astype(BF16)
        return (x, _dot_nt(h, wga_ref[...]), _dot_nt(h, wgb_ref[...]),
                _dot(oa_ref[rows, :], wba_ref[...]), _dot(ob_ref[rows, :], wbb_ref[...]))

    def finish(rows, x, ga, gb, ma, mb):
        merged = jax.nn.sigmoid(ga) * ma + jax.nn.sigmoid(gb) * mb
        o_ref[rows, :] = x + _dot(merged.astype(BF16), wo_ref[...])

    sub = x_ref.shape[0] // MERGE_SUBTILES
    groups = [slice(r * sub, (r + 1) * sub) for r in range(MERGE_SUBTILES)]
    nxt = branch_dots(groups[0])
    for r, rows in enumerate(groups):
        cur = nxt
        if r + 1 < len(groups):
            nxt = branch_dots(groups[r + 1])
        finish(rows, *cur)


def _merge(x, g, oa, ob, wga, wgb, wba, wbb, wo):
    t = x.shape[0]
    tm = TOKEN_TILE

    def row(w):
        return pl.BlockSpec((tm, w), lambda i: (i, 0))

    return pl.pallas_call(
        _merge_kernel,
        grid=(t // tm,),
        in_specs=[row(D_MODEL), _const_spec((1, D_MODEL)), row(WIDTH_A), row(WIDTH_B),
                  _const_spec((D_MODEL, D_MODEL)), _const_spec((D_MODEL, D_MODEL)),
                  _const_spec((WIDTH_A, D_MODEL)), _const_spec((WIDTH_B, D_MODEL)),
                  _const_spec((D_MODEL, D_MODEL))],
        out_specs=row(D_MODEL),
        out_shape=jax.ShapeDtypeStruct((t, D_MODEL), F32),
        compiler_params=_params(1),
        name="merge",
    )(x, g, oa, ob, wga, wgb, wba, wbb, wo)


def kernel(x, positions, ffn1_norm, ffn1_w_gate, ffn1_w_up, ffn1_w_down, mix_norm, w_in, swa_q_norm, swa_k_norm, swa_sinks, mla_q_lora_norm, mla_w_uq, mla_kv_lora_norm, mla_w_ukv, mla_q_norm, mla_k_norm, w_branch_a, w_branch_b, w_out, ffn2_norm, ffn2_w_gate, ffn2_w_up, ffn2_w_down):
    b, s, d = x.shape
    t = b * s
    depth = w_in.shape[0]
    assert d == D_MODEL and s % ATTN_BLOCK_B == 0 and t % TOKEN_TILE == 0 and t % min(ROPE_TILE, t) == 0
    assert s % (SWA_BLOCKS_PER_STEP * WINDOW) == 0

    cos_t, sin_t = _rope_tables(positions.reshape(1, t))
    pos_col3 = positions.reshape(b, s, 1)
    pos_blocks = positions.reshape(b, s // WINDOW, 1, WINDOW)

    ffn1_stack = (ffn1_w_gate, ffn1_w_up, ffn1_w_down)
    ffn2_stack = (ffn2_w_gate, ffn2_w_up, ffn2_w_down)
    ffn_w = tuple(w[0].astype(BF16) for w in ffn1_stack)
    w_in_t = jnp.swapaxes(w_in, 1, 2)

    def head_slots(r):
        return jnp.where(r < N_HEADS_A, r // 2 + GROUP_A * (r % 2), r)

    xt = x.reshape(t, d)
    for l in range(depth):
        def mixer_jobs(n_steps, l=l):
            return ([_cast_job(w, l, n_steps) for w in ffn2_stack]
                    + [_cast_job(w_in_t, l, n_steps, OFF_QA, SWA_IN, blk=HEAD_DIM_A, row_block=head_slots),
                       _cast_job(w_in_t, l, n_steps, OFF_CQ, MLA_IN),
                       _cast_job(w_in_t, l, n_steps, OFF_GA, D_MODEL),
                       _cast_job(w_in_t, l, n_steps, OFF_GB, D_MODEL),
                       _cast_job(w_branch_a, l, n_steps, blk=HEAD_DIM_A, row_block=head_slots),
                       _cast_job(w_branch_b, l, n_steps), _cast_job(w_out, l, n_steps)])

        xt, (*ffn_w, wa, wm, wga, wgb, wba, wbb, wo) = _ffn(xt, ffn1_norm[l].reshape(1, d), *ffn_w,
                                                            make_jobs=mixer_jobs)
        qag = (jnp.tile(swa_q_norm[l], 2) * HEAD_DIM_A ** -0.5).reshape(1, LANES)
        kag = jnp.tile(swa_k_norm[l], 2).reshape(1, LANES)
        qa, ka, va, qb, kb, vb = _mix_proj(
            xt, mix_norm[l].reshape(1, d), wa, wm,
            mla_q_lora_norm[l].reshape(-1, 1), mla_w_uq[l].T.astype(BF16),
            mla_kv_lora_norm[l].reshape(-1, 1), mla_w_ukv[l].T.astype(BF16), qag, kag,
            mla_q_norm[l].reshape(-1, 1), mla_k_norm[l].reshape(-1, 1), cos_t, sin_t)

        oa, ob = _attention(swa_sinks[l], qa.reshape(b, s, -1), ka.reshape(b, s, -1), va.reshape(b, s, -1),
                            pos_col3, pos_blocks, qb, kb, vb)

        xt = _merge(xt, mix_norm[l].reshape(1, d), oa.reshape(t, -1), ob.reshape(t, -1),
                    wga, wgb, wba, wbb, wo)

        def next_ffn_jobs(n_steps, l=l):
            return [_cast_job(w, l + 1, n_steps) for w in ffn1_stack] if l + 1 < depth else []

        xt, ffn_w = _ffn(xt, ffn2_norm[l].reshape(1, d), *ffn_w, make_jobs=next_ffn_jobs)
    return xt.reshape(b, s, d)
```

```python
import functools
from typing import NamedTuple

import numpy as np
import jax
import jax.numpy as jnp
from jax import lax
from jax.experimental import pallas as pl
from jax.experimental.pallas import tpu as pltpu

F32 = jnp.float32
BF16 = jnp.bfloat16

D_MODEL = 1024
D_FF = 2816
HEAD_DIM_A = 64
N_HEADS_A = 8
N_KV_HEADS_A = 2
GROUP_A = N_HEADS_A // N_KV_HEADS_A
WINDOW = 128
N_HEADS_B = 8
Q_LORA_RANK = 256
KV_LORA_RANK = 128
QK_NOPE_DIM = 64
QK_ROPE_DIM = 32
QK_DIM_B = QK_NOPE_DIM + QK_ROPE_DIM
V_DIM_B = 64
ROPE_BASE = 10000.0
WIDTH_A = N_HEADS_A * HEAD_DIM_A
WIDTH_B = N_HEADS_B * V_DIM_B
EPS = 1e-6
NEG = -1e30
MASK_DIST = 1e30

LANES = 128
HALF = LANES // 2
BF16_SUBLANES = 16

OFF_QA = 0
OFF_KA = OFF_QA + WIDTH_A
OFF_VA = OFF_KA + N_KV_HEADS_A * HEAD_DIM_A
OFF_CQ = OFF_VA + N_KV_HEADS_A * HEAD_DIM_A
OFF_CKV = OFF_CQ + Q_LORA_RANK
OFF_KR = OFF_CKV + KV_LORA_RANK
OFF_GA = OFF_KR + QK_ROPE_DIM
OFF_GB = OFF_GA + D_MODEL
IN_WIDTH = OFF_GB + D_MODEL

TOKEN_TILE = 1024
FF_CHUNK = 256
ATTN_BLOCK_B = 512
MLA_PAIRS_PER_STEP = 2
ROPE_TILE = 4096
VMEM_LIMIT = 56 * 1024 * 1024
CAST_SLABS = 16
ROPE_HALF = QK_ROPE_DIM // 2
SWA_IN = OFF_CQ
MLA_IN = Q_LORA_RANK + KV_LORA_RANK + QK_ROPE_DIM

_SWA_HEAD_ORDER = np.array([h for j in range(GROUP_A) for h in (j, j + GROUP_A)])


def _rms(x, g, n):
    ssq = jnp.sum(x * x, axis=-1, keepdims=True)
    return x * lax.rsqrt(ssq / n + EPS) * g


def _rms_rows(c, g, n):
    ssq = jnp.sum(c * c, axis=0, keepdims=True)
    return c * lax.rsqrt(ssq / n + EPS) * g


def _dot(a, b):
    return jnp.dot(a, b, preferred_element_type=F32)


def _dot_nt(a, b):
    return lax.dot_general(a, b, (((1,), (1,)), ((), ())), preferred_element_type=F32)


def _const_spec(shape):
    return pl.BlockSpec(shape, lambda *_: (0,) * len(shape), pipeline_mode=pl.Buffered(1))


def _params(n_axes):
    return pltpu.CompilerParams(dimension_semantics=("parallel",) * n_axes,
                                vmem_limit_bytes=VMEM_LIMIT)


class _CastJob(NamedTuple):
    src: jax.Array
    in_spec: pl.BlockSpec
    out_spec: pl.BlockSpec
    out_shape: tuple


def _cast_job(src, layer, n_steps, row0=0, n_rows=None, blk=None, row_block=lambda r: r):
    _, rows, cols = src.shape
    n_rows = rows - row0 if n_rows is None else n_rows
    if blk is None:
        blk = n_rows // CAST_SLABS
        if n_rows % CAST_SLABS or blk % BF16_SUBLANES or row0 % blk or CAST_SLABS > n_steps:
            blk = next(r for r in range(BF16_SUBLANES, n_rows + 1, BF16_SUBLANES)
                       if n_rows % r == 0 and row0 % r == 0 and n_rows // r <= n_steps)
    assert n_rows % blk == 0 and row0 % blk == 0 and n_rows // blk <= n_steps
    last = n_rows // blk - 1
    first = row0 // blk
    return _CastJob(src,
                    pl.BlockSpec((None, blk, cols),
                                 lambda i: (layer, first + row_block(jnp.minimum(i, last)), 0)),
                    pl.BlockSpec((blk, cols), lambda i: (jnp.minimum(i, last), 0)),
                    (n_rows, cols))


def _ffn_kernel(*refs, n_jobs):
    x_ref, g_ref, wg_ref, wu_ref, wd_ref = refs[:5]
    o_ref = refs[5 + n_jobs]
    x = x_ref[...]
    h = _rms(x, g_ref[...], D_MODEL).astype(BF16)
    y = None
    for c in range(D_FF // FF_CHUNK):
        sl = slice(c * FF_CHUNK, (c + 1) * FF_CHUNK)
        gate = _dot(h, wg_ref[:, sl])
        up = _dot(h, wu_ref[:, sl])
        act = (gate * jax.nn.sigmoid(gate) * up).astype(BF16)
        d = _dot(act, wd_ref[sl, :])
        y = d if y is None else y + d
    o_ref[...] = x + 0.5 * y
    for src_ref, dst_ref in zip(refs[5:5 + n_jobs], refs[6 + n_jobs:]):
        dst_ref[...] = src_ref[...].astype(BF16)


def _ffn(x, g, wg, wu, wd, make_jobs=lambda n_steps: ()):
    t = x.shape[0]
    tm = TOKEN_TILE
    n_steps = t // tm
    jobs = tuple(make_jobs(n_steps))
    row = pl.BlockSpec((tm, D_MODEL), lambda i: (i, 0))
    out = pl.pallas_call(
        functools.partial(_ffn_kernel, n_jobs=len(jobs)),
        grid=(n_steps,),
        in_specs=[row, _const_spec((1, D_MODEL)), _const_spec((D_MODEL, D_FF)),
                  _const_spec((D_MODEL, D_FF)), _const_spec((D_FF, D_MODEL))] + [j.in_spec for j in jobs],
        out_specs=[row] + [j.out_spec for j in jobs],
        out_shape=[jax.ShapeDtypeStruct((t, D_MODEL), F32)]
        + [jax.ShapeDtypeStruct(j.out_shape, BF16) for j in jobs],
        compiler_params=pltpu.CompilerParams(dimension_semantics=("arbitrary",),
                                             vmem_limit_bytes=VMEM_LIMIT),
        name="ffn",
    )(x, g, wg, wu, wd, *[j.src for j in jobs])
    return out[0], tuple(out[1:])


def _rope_kernel(pos_ref, invf_ref, c_ref, s_ref):
    ang = pos_ref[...].astype(F32) * invf_ref[...]
    c_ref[...] = jnp.cos(ang)
    s_ref[...] = jnp.sin(ang)


def _rope_tables(pos_row):
    t = pos_row.shape[1]
    tm = min(ROPE_TILE, t)
    inv_freq = ROPE_BASE ** (-jnp.arange(ROPE_HALF, dtype=F32) / ROPE_HALF)
    out = jax.ShapeDtypeStruct((ROPE_HALF, t), F32)
    return pl.pallas_call(
        _rope_kernel,
        grid=(t // tm,),
        in_specs=[pl.BlockSpec((1, tm), lambda i: (0, i)), _const_spec((ROPE_HALF, 1))],
        out_specs=[pl.BlockSpec((ROPE_HALF, tm), lambda i: (0, i))] * 2,
        out_shape=[out, out],
        compiler_params=_params(1),
        name="rope_tables",
    )(pos_row, inv_freq.reshape(ROPE_HALF, 1))


def _mix_proj_kernel(x_ref, g_ref, wa_ref, wm_ref, cqn_ref, wuq_ref, ckvn_ref, wukv_ref,
                     qag_ref, kag_ref, qbg_ref, kbg_ref, cos_ref, sin_ref,
                     qa_ref, ka_ref, va_ref, qb_ref, kb_ref, vb_ref):
    x = x_ref[...]
    h = _rms(x, g_ref[...], D_MODEL).astype(BF16)

    pa = _dot_nt(h, wa_ref[...])
    lane = lax.broadcasted_iota(jnp.int32, (1, LANES), 1)
    lo = lane < HALF

    def half_norm(c, gain):
        sq = c * c
        s_lo = jnp.sum(jnp.where(lo, sq, 0.0), axis=-1, keepdims=True)
        s_hi = jnp.sum(jnp.where(lo, 0.0, sq), axis=-1, keepdims=True)
        inv = jnp.where(lo, lax.rsqrt(s_lo / HEAD_DIM_A + EPS), lax.rsqrt(s_hi / HEAD_DIM_A + EPS))
        return c * inv * gain

    for j in range(WIDTH_A // LANES):
        c = pa[:, OFF_QA + j * LANES: OFF_QA + (j + 1) * LANES]
        qa_ref[:, j * LANES:(j + 1) * LANES] = half_norm(c, qag_ref[...]).astype(BF16)
    ka_ref[...] = half_norm(pa[:, OFF_KA:OFF_VA], kag_ref[...]).astype(BF16)
    va_ref[...] = pa[:, OFF_VA:OFF_CQ].astype(BF16)

    pm = _dot_nt(wm_ref[...], h)
    cq = _rms_rows(pm[0:Q_LORA_RANK], cqn_ref[...], Q_LORA_RANK).astype(BF16)
    ckv = _rms_rows(pm[Q_LORA_RANK:Q_LORA_RANK + KV_LORA_RANK], ckvn_ref[...], KV_LORA_RANK).astype(BF16)
    kr = pm[Q_LORA_RANK + KV_LORA_RANK:MLA_IN]
    qup = _dot(wuq_ref[...], cq)
    kvup = _dot(wukv_ref[...], ckv)
    cos = cos_ref[...]
    sin = sin_ref[...]

    def norm_rope(c, gain):
        y = _rms_rows(c, gain, QK_DIM_B)
        y1 = y[QK_NOPE_DIM:QK_NOPE_DIM + ROPE_HALF]
        y2 = y[QK_NOPE_DIM + ROPE_HALF:QK_DIM_B]
        return jnp.concatenate([y[0:QK_NOPE_DIM], y1 * cos - y2 * sin, y2 * cos + y1 * sin],
                               axis=0).astype(BF16)

    per_head = QK_NOPE_DIM + V_DIM_B
    for hd in range(N_HEADS_B):
        qb_ref[hd] = norm_rope(qup[hd * QK_DIM_B:(hd + 1) * QK_DIM_B], qbg_ref[...])
        k_nope = kvup[hd * per_head:hd * per_head + QK_NOPE_DIM]
        kb_ref[hd] = norm_rope(jnp.concatenate([k_nope, kr], axis=0), kbg_ref[...])
        v = kvup[hd * per_head + QK_NOPE_DIM:(hd + 1) * per_head]
        vb_ref[hd // 2, (hd % 2) * V_DIM_B:(hd % 2 + 1) * V_DIM_B, :] = v.astype(BF16)


def _mix_proj(x, g, wa, wm, cqn, wuq, ckvn, wukv, qag, kag, qbg, kbg, cos_t, sin_t):
    t = x.shape[0]
    tm = TOKEN_TILE

    def row(w):
        return pl.BlockSpec((tm, w), lambda i: (i, 0))

    def cols(*lead):
        return pl.BlockSpec((*lead, tm), lambda i: (*(0,) * len(lead), i))

    qk_shape = jax.ShapeDtypeStruct((N_HEADS_B, QK_DIM_B, t), BF16)
    return pl.pallas_call(
        _mix_proj_kernel,
        grid=(t // tm,),
        in_specs=[row(D_MODEL), _const_spec((1, D_MODEL)),
                  _const_spec((SWA_IN, D_MODEL)), _const_spec((MLA_IN, D_MODEL)),
                  _const_spec((Q_LORA_RANK, 1)), _const_spec((N_HEADS_B * QK_DIM_B, Q_LORA_RANK)),
                  _const_spec((KV_LORA_RANK, 1)),
                  _const_spec((N_HEADS_B * (QK_NOPE_DIM + V_DIM_B), KV_LORA_RANK)),
                  _const_spec((1, LANES)), _const_spec((1, LANES)),
                  _const_spec((QK_DIM_B, 1)), _const_spec((QK_DIM_B, 1)),
                  cols(ROPE_HALF), cols(ROPE_HALF)],
        out_specs=[row(WIDTH_A), row(LANES), row(LANES),
                   cols(N_HEADS_B, QK_DIM_B), cols(N_HEADS_B, QK_DIM_B), cols(N_HEADS_B // 2, LANES)],
        out_shape=[jax.ShapeDtypeStruct((t, w), BF16) for w in (WIDTH_A, LANES, LANES)]
        + [qk_shape, qk_shape, jax.ShapeDtypeStruct((N_HEADS_B // 2, LANES, t), BF16)],
        compiler_params=_params(1),
        name="mix_proj",
    )(x, g, wa, wm, cqn, wuq, ckvn, wukv, qag, kag, qbg, kbg, cos_t, sin_t)


SWA_BLOCKS_PER_STEP = 8


def _swa_blocks(step, sink_ref, q_ref, k_ref, v_ref, pq_ref, pk_ref, o_ref):
    blk = WINDOW
    lane = lax.broadcasted_iota(jnp.int32, (1, LANES), 1)
    lo = lane < HALF
    qi = lax.broadcasted_iota(jnp.int32, (blk, 2 * blk), 0) + blk
    ki = lax.broadcasted_iota(jnp.int32, (blk, 2 * blk), 1)
    diff = qi - ki
    band = (diff >= 0) & (diff < WINDOW)
    zero = jnp.zeros((), BF16)

    pending = {}

    def scores(r):
        cur = step * SWA_BLOCKS_PER_STEP + r
        rows = slice(r * blk, (r + 1) * blk)
        if r == 0:
            prev = jnp.maximum(cur - 1, 0)
            p0 = pl.multiple_of(prev * blk, blk)
            c0 = pl.multiple_of(cur * blk, blk)
            kk = jnp.concatenate([k_ref[0, pl.ds(p0, blk), :], k_ref[0, pl.ds(c0, blk), :]], axis=0)
            vv = jnp.concatenate([v_ref[0, pl.ds(p0, blk), :], v_ref[0, pl.ds(c0, blk), :]], axis=0)
            valid = band & ((ki >= blk) | (step > 0))
        else:
            prev = cur - 1
            p0 = pl.multiple_of(prev * blk, blk)
            kk = k_ref[0, pl.ds(p0, 2 * blk), :]
            vv = v_ref[0, pl.ds(p0, 2 * blk), :]
            valid = band
        pk = jnp.concatenate([pk_ref[0, prev], pk_ref[0, cur]], axis=1)
        dist = jnp.where(valid, (pq_ref[0, rows, :] - pk).astype(F32), MASK_DIST)
        s_kv = []
        for kv in range(N_KV_HEADS_A):
            keep = lo if kv == 0 else jnp.logical_not(lo)
            qs = [jnp.where(keep, q_ref[0, rows, j * LANES:(j + 1) * LANES], zero) for j in range(GROUP_A)]
            s_kv.append(_dot_nt(jnp.concatenate(qs, axis=0), kk))
        pending[r] = (s_kv, vv, dist)

    def finish(r):
        rows = slice(r * blk, (r + 1) * blk)
        s_kv, vv, dist = pending.pop(r)
        o_kv, sink_kv = [], []
        for kv in range(N_KV_HEADS_A):
            keep = lo if kv == 0 else jnp.logical_not(lo)
            s_all = s_kv[kv]
            ps, sink_terms = [], []
            for j in range(GROUP_A):
                hd = kv * GROUP_A + j
                slope = 2.0 ** (-8.0 * (hd + 1) / N_HEADS_A)
                sink = sink_ref[hd]
                s = s_all[j * blk:(j + 1) * blk] - slope * dist
                m = jnp.maximum(jnp.max(s, axis=-1, keepdims=True), sink)
                ps.append(jnp.exp(s - m).astype(BF16))
                sink_terms.append(jnp.exp(sink - m))
            v_ext = jnp.where(keep, vv, jnp.ones((), BF16))
            o_kv.append(_dot(jnp.concatenate(ps, axis=0), v_ext))
            sink_kv.append(sink_terms)
        for j in range(GROUP_A):
            o0 = o_kv[0][j * blk:(j + 1) * blk]
            o1 = o_kv[1][j * blk:(j + 1) * blk]
            num = jnp.where(lo, o0, o1)
            den = pltpu.roll(jnp.where(lo, o1, o0), HALF, 1) + jnp.where(lo, sink_kv[0][j], sink_kv[1][j])
            o_ref[0, rows, j * LANES:(j + 1) * LANES] = (num / den).astype(BF16)

    scores(0)
    items = []
    for r in range(SWA_BLOCKS_PER_STEP):
        def item(r=r):
            if r + 1 < SWA_BLOCKS_PER_STEP:
                scores(r + 1)
            finish(r)

        items.append(item)
    return items


def _mla_units(q_ref, k_ref, v_ref, o_ref):
    s_len = q_ref.shape[2]
    tq = ATTN_BLOCK_B
    c = QK_DIM_B ** -0.5 * np.log2(np.e)
    causal = (lax.broadcasted_iota(jnp.int32, (tq, tq), 0)
              >= lax.broadcasted_iota(jnp.int32, (tq, tq), 1))
    ones = jnp.ones((V_DIM_B, s_len), BF16)
    lo = lax.broadcasted_iota(jnp.int32, (1, LANES), 1) < HALF

    v_exts = []
    for pair in range(MLA_PAIRS_PER_STEP):
        v = v_ref[pair]
        v_exts.append((jnp.concatenate([v[0:V_DIM_B], ones], axis=0),
                       jnp.concatenate([ones, v[V_DIM_B:]], axis=0)))

    pending = {}

    def qk(pair, i):
        n = (i + 1) * tq
        scores = []
        for half in range(2):
            hd = 2 * pair + half
            q = q_ref[hd, :, i * tq:n].T
            scores.append(_dot(q, k_ref[hd, :, 0:n]))
        pending[pair, i] = scores

    def finish(pair, i):
        n = (i + 1) * tq
        probs = []
        for s in pending.pop((pair, i)):
            diag = jnp.where(causal, s[:, n - tq:], NEG)
            s = diag if i == 0 else jnp.concatenate([s[:, :n - tq], diag], axis=1)
            m = jnp.max(s, axis=-1, keepdims=True)
            probs.append(jnp.exp2((s - m) * c).astype(BF16))
        o = [_dot_nt(p, v_exts[pair][half][:, 0:n]) for half, p in enumerate(probs)]
        num = jnp.where(lo, o[0], o[1])
        den = pltpu.roll(jnp.where(lo, o[1], o[0]), HALF, 1)
        o_ref[0, i * tq:n, pair * LANES:(pair + 1) * LANES] = (num / den).astype(BF16)

    order = [(pair, i) for i in reversed(range(s_len // tq)) for pair in range(MLA_PAIRS_PER_STEP)]
    qk(*order[0])
    items = []
    for k, (pair, i) in enumerate(order):
        nxt = order[k + 1] if k + 1 < len(order) else None

        def item(pair=pair, i=i, nxt=nxt):
            if nxt is not None:
                qk(*nxt)
            finish(pair, i)

        items.append(item)
    return items


def _attn_kernel(sink_ref, qa_ref, ka_ref, va_ref, pq_ref, pk_ref, qb_ref, kb_ref, vb_ref, oa_ref, ob_ref):
    swa = _swa_blocks(pl.program_id(1), sink_ref, qa_ref, ka_ref, va_ref, pq_ref, pk_ref, oa_ref)
    mla = _mla_units(qb_ref, kb_ref, vb_ref, ob_ref)
    for k in range(max(len(swa), len(mla))):
        for items in (mla, swa):
            if k < len(items):
                items[k]()


def _attention(sinks, qa, ka, va, pos_col, pos_blocks, qb, kb, vb):
    b, s, _ = qa.shape
    rows = SWA_BLOCKS_PER_STEP * WINDOW
    pp = MLA_PAIRS_PER_STEP
    assert s // rows == N_HEADS_B // (2 * pp)
    cur = lambda bi, j: (bi, j, 0)
    whole = lambda bi, j: (bi, 0, 0)
    qk_spec = pl.BlockSpec((2 * pp, QK_DIM_B, s), lambda bi, j: (j, 0, bi))
    out = jax.ShapeDtypeStruct((b, s, WIDTH_A), BF16)
    return pl.pallas_call(
        _attn_kernel,
        grid=(b, s // rows),
        in_specs=[pl.BlockSpec(memory_space=pltpu.SMEM),
                  pl.BlockSpec((1, rows, WIDTH_A), cur),
                  pl.BlockSpec((1, s, LANES), whole), pl.BlockSpec((1, s, LANES), whole),
                  pl.BlockSpec((1, rows, 1), cur),
                  pl.BlockSpec((1, s // WINDOW, 1, WINDOW), lambda bi, j: (bi, 0, 0, 0)),
                  qk_spec, qk_spec, pl.BlockSpec((pp, LANES, s), lambda bi, j: (j, 0, bi))],
        out_specs=[pl.BlockSpec((1, rows, WIDTH_A), cur),
                   pl.BlockSpec((1, s, pp * LANES), lambda bi, j: (bi, 0, j))],
        out_shape=[out, out],
        compiler_params=_params(2),
        name="attention",
    )(sinks, qa, ka, va, pos_col, pos_blocks, qb, kb, vb)


def _merge_kernel(x_ref, g_ref, oa_ref, ob_ref, wga_ref, wgb_ref, wba_ref, wbb_ref, wo_ref, o_ref):
    x = x_ref[...]
    h = _rms(x, g_ref[...], D_MODEL).astype(BF16)
    ga = _dot_nt(h, wga_ref[...])
    gb = _dot_nt(h, wgb_ref[...])
    ma = _dot(oa_ref[...], wba_ref[...])
    mb = _dot(ob_ref[...], wbb_ref[...])
    merged = jax.nn.sigmoid(ga) * ma + jax.nn.sigmoid(gb) * mb
    o_ref[...] = x + _dot(merged.astype(BF16), wo_ref[...])


def _merge(x, g, oa, ob, wga, wgb, wba, wbb, wo):
    t = x.shape[0]
    tm = TOKEN_TILE

    def row(w):
        return pl.BlockSpec((tm, w), lambda i: (i, 0))

    return pl.pallas_call(
        _merge_kernel,
        grid=(t // tm,),
        in_specs=[row(D_MODEL), _const_spec((1, D_MODEL)), row(WIDTH_A), row(WIDTH_B),
                  _const_spec((D_MODEL, D_MODEL)), _const_spec((D_MODEL, D_MODEL)),
                  _const_spec((WIDTH_A, D_MODEL)), _const_spec((WIDTH_B, D_MODEL)),
                  _const_spec((D_MODEL, D_MODEL))],
        out_specs=row(D_MODEL),
        out_shape=jax.ShapeDtypeStruct((t, D_MODEL), F32),
        compiler_params=_params(1),
        name="merge",
    )(x, g, oa, ob, wga, wgb, wba, wbb, wo)


def kernel(x, positions, ffn1_norm, ffn1_w_gate, ffn1_w_up, ffn1_w_down, mix_norm, w_in, swa_q_norm, swa_k_norm, swa_sinks, mla_q_lora_norm, mla_w_uq, mla_kv_lora_norm, mla_w_ukv, mla_q_norm, mla_k_norm, w_branch_a, w_branch_b, w_out, ffn2_norm, ffn2_w_gate, ffn2_w_up, ffn2_w_down):
    b, s, d = x.shape
    t = b * s
    depth = w_in.shape[0]
    assert d == D_MODEL and s % ATTN_BLOCK_B == 0 and t % TOKEN_TILE == 0 and t % min(ROPE_TILE, t) == 0
    assert s % (SWA_BLOCKS_PER_STEP * WINDOW) == 0

    cos_t, sin_t = _rope_tables(positions.reshape(1, t))
    pos_col3 = positions.reshape(b, s, 1)
    pos_blocks = positions.reshape(b, s // WINDOW, 1, WINDOW)

    ffn1_stack = (ffn1_w_gate, ffn1_w_up, ffn1_w_down)
    ffn2_stack = (ffn2_w_gate, ffn2_w_up, ffn2_w_down)
    ffn_w = tuple(w[0].astype(BF16) for w in ffn1_stack)
    w_in_t = jnp.swapaxes(w_in, 1, 2)

    def head_slots(r):
        return jnp.where(r < N_HEADS_A, r // 2 + GROUP_A * (r % 2), r)

    xt = x.reshape(t, d)
    for l in range(depth):
        def mixer_jobs(n_steps, l=l):
            return ([_cast_job(w, l, n_steps) for w in ffn2_stack]
                    + [_cast_job(w_in_t, l, n_steps, OFF_QA, SWA_IN, blk=HEAD_DIM_A, row_block=head_slots),
                       _cast_job(w_in_t, l, n_steps, OFF_CQ, MLA_IN),
                       _cast_job(w_in_t, l, n_steps, OFF_GA, D_MODEL),
                       _cast_job(w_in_t, l, n_steps, OFF_GB, D_MODEL),
                       _cast_job(w_branch_a, l, n_steps, blk=HEAD_DIM_A, row_block=head_slots),
                       _cast_job(w_branch_b, l, n_steps), _cast_job(w_out, l, n_steps)])

        xt, (*ffn_w, wa, wm, wga, wgb, wba, wbb, wo) = _ffn(xt, ffn1_norm[l].reshape(1, d), *ffn_w,
                                                            make_jobs=mixer_jobs)
        qag = (jnp.tile(swa_q_norm[l], 2) * HEAD_DIM_A ** -0.5).reshape(1, LANES)
        kag = jnp.tile(swa_k_norm[l], 2).reshape(1, LANES)
        qa, ka, va, qb, kb, vb = _mix_proj(
            xt, mix_norm[l].reshape(1, d), wa, wm,
            mla_q_lora_norm[l].reshape(-1, 1), mla_w_uq[l].T.astype(BF16),
            mla_kv_lora_norm[l].reshape(-1, 1), mla_w_ukv[l].T.astype(BF16), qag, kag,
            mla_q_norm[l].reshape(-1, 1), mla_k_norm[l].reshape(-1, 1), cos_t, sin_t)

        oa, ob = _attention(swa_sinks[l], qa.reshape(b, s, -1), ka.reshape(b, s, -1), va.reshape(b, s, -1),
                            pos_col3, pos_blocks, qb, kb, vb)

        xt = _merge(xt, mix_norm[l].reshape(1, d), oa.reshape(t, -1), ob.reshape(t, -1),
                    wga, wgb, wba, wbb, wo)

        def next_ffn_jobs(n_steps, l=l):
            return [_cast_job(w, l + 1, n_steps) for w in ffn1_stack] if l + 1 < depth else []

        xt, ffn_w = _ffn(xt, ffn2_norm[l].reshape(1, d), *ffn_w, make_jobs=next_ffn_jobs)
    return xt.reshape(b, s, d)
```
